```python
import math
import jax, jax.numpy as jnp
from jax import lax
import numpy as np

D_MODEL = 1024
BATCH = 4
SEQ = 8192
DEPTH = 1

CHUNK = 64
Q_BLOCK = 128
N_MEM = 256
ROPE_THETA = 10000.0
EPS = 1e-6

DA_HEADS = 6
DA_QK_DIM = 64
DA_V_DIM = 2 * DA_QK_DIM
GDN_HEADS = 6
GDN_K_DIM = 128
GDN_V_DIM = 128
CONV_K = 4
CA_HEADS = 4
CA_DIM = 128
D_FF = 4 * D_MODEL
N_BRANCH = 3

DA_Q = DA_HEADS * 2 * DA_QK_DIM
DA_V = DA_HEADS * DA_V_DIM
GDN_K = GDN_HEADS * GDN_K_DIM
GDN_V = GDN_HEADS * GDN_V_DIM
CA_Q = CA_HEADS * CA_DIM
IN_SPLITS = (DA_Q, DA_Q, DA_V, GDN_K, GDN_K, GDN_V, GDN_V, GDN_HEADS, GDN_HEADS, CA_Q)
D_IN = sum(IN_SPLITS)

kernel_name = 'hybrid_diffattn_gdn_memxattn_gated_block'


def rms_norm(x, g):
    xf = x.astype(jnp.float32)
    y = xf * lax.rsqrt(jnp.mean(xf * xf, axis=-1, keepdims=True) + EPS)
    return (y * g.astype(jnp.float32)).astype(x.dtype)


def l2_norm(x):
    xf = x.astype(jnp.float32)
    return xf * lax.rsqrt(jnp.sum(xf * xf, axis=-1, keepdims=True) + EPS)


def rope(x, positions):
    d = x.shape[-1]
    half = d // 2
    inv = jnp.exp(-math.log(ROPE_THETA) * jnp.arange(half, dtype=jnp.float32) / half)
    ang = positions.astype(jnp.float32)[..., None] * inv
    ang = ang.reshape(ang.shape[:2] + (1,) * (x.ndim - 3) + (half,))
    cos, sin = jnp.cos(ang), jnp.sin(ang)
    xf = x.astype(jnp.float32)
    x1, x2 = xf[..., :half], xf[..., half:]
    return jnp.concatenate([x1 * cos - x2 * sin, x2 * cos + x1 * sin], axis=-1).astype(x.dtype)


def diff_attention(q, k, v, lam):
    B, S = q.shape[:2]
    qh = q.transpose(0, 2, 3, 1, 4)
    kh = k.transpose(0, 2, 3, 1, 4)
    vh = v.transpose(0, 2, 1, 3)
    key_chunk = jnp.arange(S) // CHUNK
    scale = DA_QK_DIM ** -0.5

    def block(i):
        start = i * Q_BLOCK
        qb = lax.dynamic_slice_in_dim(qh, start, Q_BLOCK, axis=3)
        s = jnp.einsum('bhmqd,bhmkd->bhmqk', qb, kh).astype(jnp.float32) * scale
        q_chunk = (start + jnp.arange(Q_BLOCK)) // CHUNK
        mask = key_chunk[None, :] <= q_chunk[:, None]
        p = jax.nn.softmax(jnp.where(mask, s, -jnp.inf), axis=-1)
        a = p[:, :, 0] - lam * p[:, :, 1]
        return jnp.einsum('bhqk,bhkd->bhqd', a.astype(vh.dtype), vh)

    o = lax.map(block, jnp.arange(S // Q_BLOCK))
    return o.transpose(1, 0, 3, 2, 4).reshape(B, S, DA_HEADS, DA_V_DIM)


def causal_depthwise_conv(x, w):
    C = x.shape[-1]
    return lax.conv_general_dilated(
        x, w[:, None, :].astype(x.dtype), window_strides=(1,),
        padding=[(CONV_K - 1, 0)], dimension_numbers=('NWC', 'WIO', 'NWC'),
        feature_group_count=C)


def gated_delta_chunked(q, k, v, log_alpha, beta):
    B, S, H, Dk = q.shape
    Dv = v.shape[-1]
    N = S // CHUNK

    def to_chunks(t):
        return t.astype(jnp.float32).reshape(B, N, CHUNK, H, -1).transpose(0, 3, 1, 2, 4)

    qc = to_chunks(q) * (Dk ** -0.5)
    kc = to_chunks(k)
    vc = to_chunks(v)
    g = jnp.cumsum(log_alpha.astype(jnp.float32).reshape(B, N, CHUNK, H).transpose(0, 3, 1, 2), axis=-1)
    bt = beta.astype(jnp.float32).reshape(B, N, CHUNK, H).transpose(0, 3, 1, 2)

    incl = jnp.tril(jnp.ones((CHUNK, CHUNK), dtype=bool))
    strict = jnp.tril(jnp.ones((CHUNK, CHUNK), dtype=bool), -1)
    gdiff = g[..., :, None] - g[..., None, :]
    decay = jnp.where(incl, jnp.exp(jnp.where(incl, gdiff, 0.0)), 0.0)

    kk = jnp.einsum('bhncd,bhnjd->bhncj', kc, kc)
    A = jnp.where(strict, bt[..., None] * kk * decay, 0.0)
    eye = jnp.eye(CHUNK, dtype=jnp.float32)
    T = lax.linalg.triangular_solve(A + eye, jnp.broadcast_to(eye, A.shape),
                                    left_side=True, lower=True, unit_diagonal=True)
    u = jnp.einsum('bhncj,bhnjd->bhncd', T, vc * bt[..., None])
    w = jnp.einsum('bhncj,bhnjd->bhncd', T, kc * (bt * jnp.exp(g))[..., None])
    qk = jnp.einsum('bhncd,bhnjd->bhncj', qc, kc) * decay
    q_dec = qc * jnp.exp(g)[..., None]
    k_dec = kc * jnp.exp(g[..., -1:] - g)[..., None]
    chunk_decay = jnp.exp(g[..., -1])

    def step(state, inp):
        u_i, w_i, qk_i, qd_i, kd_i, cd_i = inp
        v_new = u_i - jnp.einsum('bhcd,bhde->bhce', w_i, state)
        o = jnp.einsum('bhcd,bhde->bhce', qd_i, state) + jnp.einsum('bhcj,bhje->bhce', qk_i, v_new)
        state = state * cd_i[..., None, None] + jnp.einsum('bhcd,bhce->bhde', kd_i, v_new)
        return state, o

    xs = tuple(jnp.moveaxis(t, 2, 0) for t in (u, w, qk, q_dec, k_dec, chunk_decay))
    state0 = jnp.zeros((B, H, Dk, Dv), jnp.float32)
    _, o = lax.scan(step, state0, xs)
    return o.transpose(1, 0, 3, 2, 4).reshape(B, S, H, Dv).astype(q.dtype)


def setup_inputs(seed: int = 0) -> dict:
    key = jax.random.key(seed)
    ks = jax.random.split(key, 32)
    L, D = DEPTH, D_MODEL

    def nrm(k, shape, scale):
        return jax.random.normal(k, shape, jnp.float32) * scale

    def gain(k, n):
        return 1.0 + nrm(k, (L, n), 0.02)

    x = nrm(ks[0], (BATCH, SEQ, D), 1.0)
    mem = nrm(ks[1], (BATCH, N_MEM, D), 1.0)
    offsets = jax.random.randint(ks[2], (BATCH, 1), 0, 256) * CHUNK
    positions = (offsets + jnp.arange(SEQ, dtype=jnp.int32)[None, :]).astype(jnp.int32)
    a_log = jnp.log(jax.random.uniform(ks[3], (L, GDN_HEADS), jnp.float32, 1.0, 16.0))
    dt = jnp.exp(jax.random.uniform(ks[4], (L, GDN_HEADS), jnp.float32, math.log(1e-3), math.log(1e-1)))
    dt_bias = dt + jnp.log(-jnp.expm1(-dt))
    return {
        'x': x,
        'mem': mem,
        'positions': positions,
        'ln1_g': gain(ks[5], D),
        'w_in': nrm(ks[6], (L, D, D_IN), D ** -0.5),
        'w_gate': nrm(ks[7], (L, D, N_BRANCH * D), D ** -0.5),
        'b_gate': nrm(ks[8], (L, N_BRANCH * D), 0.1),
        'da_qnorm_g': gain(ks[9], DA_QK_DIM),
        'da_knorm_g': gain(ks[10], DA_QK_DIM),
        'da_lambda_q1': nrm(ks[11], (L, DA_QK_DIM), 0.1),
        'da_lambda_k1': nrm(ks[12], (L, DA_QK_DIM), 0.1),
        'da_lambda_q2': nrm(ks[13], (L, DA_QK_DIM), 0.1),
        'da_lambda_k2': nrm(ks[14], (L, DA_QK_DIM), 0.1),
        'da_subln_g': gain(ks[15], DA_V_DIM),
        'w_o_diff': nrm(ks[16], (L, DA_V, D), DA_V ** -0.5),
        'w_conv': nrm(ks[17], (L, CONV_K, 2 * GDN_K + GDN_V), CONV_K ** -0.5),
        'gdn_a_log': a_log,
        'gdn_dt_bias': dt_bias,
        'gdn_norm_g': gain(ks[18], GDN_V_DIM),
        'w_o_delta': nrm(ks[19], (L, GDN_V, D), GDN_V ** -0.5),
        'mem_norm_g': gain(ks[20], D),
        'w_mem_kv': nrm(ks[21], (L, D, 2 * CA_Q), D ** -0.5),
        'ca_qnorm_g': gain(ks[22], CA_DIM),
        'ca_knorm_g': gain(ks[23], CA_DIM),
        'w_o_cross': nrm(ks[24], (L, CA_Q, D), CA_Q ** -0.5),
        'w_out': nrm(ks[25], (L, D, D), D ** -0.5),
        'ln2_g': gain(ks[26], D),
        'w_mlp1': nrm(ks[27], (L, D, D_FF), D ** -0.5),
        'w_mlp2': nrm(ks[28], (L, D_FF, D), D_FF ** -0.5),
    }


def reference(x, mem, positions, ln1_g, w_in, w_gate, b_gate, da_qnorm_g, da_knorm_g,
              da_lambda_q1, da_lambda_k1, da_lambda_q2, da_lambda_k2, da_subln_g, w_o_diff,
              w_conv, gdn_a_log, gdn_dt_bias, gdn_norm_g, w_o_delta, mem_norm_g, w_mem_kv,
              ca_qnorm_g, ca_knorm_g, w_o_cross, w_out, ln2_g, w_mlp1, w_mlp2):
    B, S, _ = x.shape
    split_points = [int(p) for p in np.cumsum(IN_SPLITS)[:-1]]
    for l in range(DEPTH):
        h = rms_norm(x, ln1_g[l])
        proj = h @ w_in[l]
        (da_q, da_k, da_v, gd_q, gd_k, gd_v, gd_z, gd_b, gd_a, ca_q) = jnp.split(proj, split_points, axis=-1)
        gates = jax.nn.sigmoid((h @ w_gate[l] + b_gate[l]).astype(jnp.float32))
        gates = gates.astype(x.dtype).reshape(B, S, N_BRANCH, D_MODEL)

        qa = rope(rms_norm(da_q.reshape(B, S, DA_HEADS, 2, DA_QK_DIM), da_qnorm_g[l]), positions)
        ka = rope(rms_norm(da_k.reshape(B, S, DA_HEADS, 2, DA_QK_DIM), da_knorm_g[l]), positions)
        va = da_v.reshape(B, S, DA_HEADS, DA_V_DIM)
        lam_init = 0.8 - 0.6 * math.exp(-0.3 * l)
        lam = (jnp.exp(jnp.sum(da_lambda_q1[l].astype(jnp.float32) * da_lambda_k1[l].astype(jnp.float32)))
               - jnp.exp(jnp.sum(da_lambda_q2[l].astype(jnp.float32) * da_lambda_k2[l].astype(jnp.float32)))
               + lam_init)
        oa = diff_attention(qa, ka, va, lam)
        oa = rms_norm(oa, da_subln_g[l]) * (1.0 - lam_init)
        ya = oa.reshape(B, S, DA_V) @ w_o_diff[l]

        qkv = jax.nn.silu(causal_depthwise_conv(jnp.concatenate([gd_q, gd_k, gd_v], axis=-1), w_conv[l]))
        cq, ck, cv = jnp.split(qkv, [GDN_K, 2 * GDN_K], axis=-1)
        cq = l2_norm(cq.reshape(B, S, GDN_HEADS, GDN_K_DIM))
        ck = l2_norm(ck.reshape(B, S, GDN_HEADS, GDN_K_DIM))
        cv = cv.reshape(B, S, GDN_HEADS, GDN_V_DIM)
        beta = jax.nn.sigmoid(gd_b.astype(jnp.float32))
        log_alpha = -jnp.exp(gdn_a_log[l].astype(jnp.float32)) * jax.nn.softplus(
            gd_a.astype(jnp.float32) + gdn_dt_bias[l].astype(jnp.float32))
        ob = gated_delta_chunked(cq, ck, cv, log_alpha, beta).astype(x.dtype)
        ob = rms_norm(ob, gdn_norm_g[l]) * jax.nn.silu(gd_z.reshape(B, S, GDN_HEADS, GDN_V_DIM))
        yb = ob.reshape(B, S, GDN_V) @ w_o_delta[l]

        m = rms_norm(mem, mem_norm_g[l])
        mk, mv = jnp.split(m @ w_mem_kv[l], 2, axis=-1)
        kc = rms_norm(mk.reshape(B, N_MEM, CA_HEADS, CA_DIM), ca_knorm_g[l])
        vc = mv.reshape(B, N_MEM, CA_HEADS, CA_DIM)
        qc = rms_norm(ca_q.reshape(B, S, CA_HEADS, CA_DIM), ca_qnorm_g[l])
        sc = jnp.einsum('bshd,bmhd->bhsm', qc, kc).astype(jnp.float32) * (CA_DIM ** -0.5)
        pc = jax.nn.softmax(sc, axis=-1)
        oc = jnp.einsum('bhsm,bmhd->bshd', pc.astype(vc.dtype), vc)
        yc = oc.reshape(B, S, CA_Q) @ w_o_cross[l]

        mixed = gates[:, :, 0] * ya + gates[:, :, 1] * yb + gates[:, :, 2] * yc
        x = x + mixed @ w_out[l]

        h2 = rms_norm(x, ln2_g[l])
        x = x + jnp.square(jax.nn.relu(h2 @ w_mlp1[l])) @ w_mlp2[l]
    return x
```

```python
import functools
import math

import jax
import jax.numpy as jnp
from jax import lax
from jax.experimental import pallas as pl
from jax.experimental.pallas import tpu as pltpu

F32 = jnp.float32
BF16 = jnp.bfloat16

EPS = 1e-6
CHUNK = 64
ROPE_THETA = 10000.0
N_BRANCH = 3

DA_HEADS = 6
DA_QK_DIM = 64
DA_V_DIM = 128
GDN_HEADS = 6
GDN_K_DIM = 128
GDN_V_DIM = 128
CONV_K = 4
CA_HEADS = 4
CA_DIM = 128

LANES = 128
HW = GDN_HEADS * GDN_K_DIM
CA_Q = CA_HEADS * CA_DIM

COL_GQ, COL_GK, COL_GV, COL_GZ, COL_DQ, COL_DK, COL_DV, COL_CQ = (
    0, HW, 2 * HW, 3 * HW, 4 * HW, 5 * HW, 6 * HW, 7 * HW)
N_PROJ = 7 * HW + CA_Q
AB_B0, AB_A0 = 0, 8

VMEM_LIMIT = 48 * 1024 * 1024


def _cparams(sem):
    return pltpu.CompilerParams(dimension_semantics=sem, vmem_limit_bytes=VMEM_LIMIT)


def _nt_dot(a, b):
    return lax.dot_general(a, b, (((1,), (1,)), ((), ())), preferred_element_type=F32)


def _tn_dot(a, b):
    return lax.dot_general(a, b, (((0,), (0,)), ((), ())), preferred_element_type=F32)


def _dot(a, b):
    return jnp.dot(a, b, preferred_element_type=F32)


def _norm_matmul_kernel(x_ref, g_ref, w_ref, b_ref, o_ref, h_ref, *, act):
    @pl.when(pl.program_id(1) == 0)
    def _():
        x = x_ref[...]
        ms = jnp.mean(x * x, axis=-1, keepdims=True)
        h_ref[...] = (x * lax.rsqrt(ms + EPS) * g_ref[...]).astype(h_ref.dtype)

    y = _dot(h_ref[...], w_ref[...]) + b_ref[...]
    if act == "sigmoid":
        y = jax.nn.sigmoid(y)
    o_ref[...] = y.astype(o_ref.dtype)


def _norm_matmul(x, g, w, b, *, act, out_dtype, tm, tn):
    t, d = x.shape
    n = w.shape[1]
    return pl.pallas_call(
        functools.partial(_norm_matmul_kernel, act=act),
        grid=(t // tm, n // tn),
        in_specs=[
            pl.BlockSpec((tm, d), lambda i, j: (i, 0)),
            pl.BlockSpec((1, d), lambda i, j: (0, 0)),
            pl.BlockSpec((d, tn), lambda i, j: (0, j)),
            pl.BlockSpec((1, tn), lambda i, j: (0, j)),
        ],
        out_specs=pl.BlockSpec((tm, tn), lambda i, j: (i, j)),
        out_shape=jax.ShapeDtypeStruct((t, n), out_dtype),
        scratch_shapes=[pltpu.VMEM((tm, d), BF16)],
        compiler_params=_cparams(("parallel", "arbitrary")),
    )(x, g, w, b)


def _qk_prep_kernel(q_ref, k_ref, pos_ref, inv_ref, gq_ref, gk_ref, qz_ref, kr_ref):
    tm = q_ref.shape[0]
    ang = pos_ref[...].astype(F32) * inv_ref[...]
    cos = jnp.cos(ang)
    sin = jnp.sin(ang)
    lane = lax.broadcasted_iota(jnp.int32, (tm, LANES), 1)
    lo_map = lane < DA_QK_DIM
    first_half = (lane % DA_QK_DIM) < (DA_QK_DIM // 2)
    sin_signed = jnp.where(first_half, -sin, sin)
    scale = DA_QK_DIM ** -0.5

    def norm_rope(x, g):
        s = x * x
        tot = jnp.sum(s, axis=-1, keepdims=True)
        lo = jnp.sum(jnp.where(lo_map, s, 0.0), axis=-1, keepdims=True)
        ms = jnp.where(lo_map, lo, tot - lo) * (1.0 / DA_QK_DIM)
        y = x * lax.rsqrt(ms + EPS) * g
        partner = jnp.where(first_half,
                            pltpu.roll(y, LANES - DA_QK_DIM // 2, axis=1),
                            pltpu.roll(y, DA_QK_DIM // 2, axis=1))
        return y * cos + partner * sin_signed

    for h in range(DA_HEADS):
        sl = slice(h * LANES, (h + 1) * LANES)
        q = norm_rope(q_ref[:, sl].astype(F32), gq_ref[...]) * scale
        qz_ref[0, :, sl] = jnp.where(lo_map, q, 0.0).astype(qz_ref.dtype)
        qz_ref[1, :, sl] = jnp.where(lo_map, 0.0, q).astype(qz_ref.dtype)
        k = norm_rope(k_ref[:, sl].astype(F32), gk_ref[...])
        kr_ref[:, sl] = k.astype(kr_ref.dtype)


def _qk_prep(proj, pos, inv, gq, gk, *, tm):
    t = proj.shape[0]
    return pl.pallas_call(
        _qk_prep_kernel,
        grid=(t // tm,),
        in_specs=[
            pl.BlockSpec((tm, HW), lambda i: (i, COL_DQ // HW)),
            pl.BlockSpec((tm, HW), lambda i: (i, COL_DK // HW)),
            pl.BlockSpec((tm, 1), lambda i: (i, 0)),
            pl.BlockSpec((1, LANES), lambda i: (0, 0)),
            pl.BlockSpec((1, LANES), lambda i: (0, 0)),
            pl.BlockSpec((1, LANES), lambda i: (0, 0)),
        ],
        out_specs=[
            pl.BlockSpec((2, tm, HW), lambda i: (0, i, 0)),
            pl.BlockSpec((tm, HW), lambda i: (i, 0)),
        ],
        out_shape=[
            jax.ShapeDtypeStruct((2, t, HW), BF16),
            jax.ShapeDtypeStruct((t, HW), BF16),
        ],
        compiler_params=_cparams(("parallel",)),
    )(proj, proj, pos, inv, gq, gk)


def _diff_attn_kernel(qz_ref, k_ref, v_ref, lam_ref, g_ref, o_ref, m_ref, l_ref, acc_ref,
                      *, tq, lam_init):
    i = pl.program_id(2)
    q = qz_ref[...].reshape(2 * tq, LANES)
    m_ref[...] = jnp.full(m_ref.shape, -jnp.inf, F32)
    l_ref[...] = jnp.zeros(l_ref.shape, F32)
    acc_ref[...] = jnp.zeros(acc_ref.shape, F32)

    def block(j, masked):
        r0 = pl.multiple_of(j * tq, tq)
        k = k_ref[pl.ds(r0, tq), :]
        v = v_ref[pl.ds(r0, tq), :]
        s = _nt_dot(q, k)
        if masked:
            row = lax.broadcasted_iota(jnp.int32, s.shape, 0)
            col = lax.broadcasted_iota(jnp.int32, s.shape, 1)
            ok = (col // CHUNK) <= ((row % tq) // CHUNK)
            s = jnp.where(ok, s, -jnp.inf)
        m_prev = m_ref[...]
        m_new = jnp.maximum(m_prev, jnp.max(s, axis=-1, keepdims=True))
        alpha = jnp.exp(m_prev - m_new)
        p = jnp.exp(s - m_new)
        l_ref[...] = alpha * l_ref[...] + jnp.sum(p, axis=-1, keepdims=True)
        acc_ref[...] = alpha * acc_ref[...] + _dot(p.astype(v.dtype), v)
        m_ref[...] = m_new

    def body(j, carry):
        block(j, False)
        return carry

    lax.fori_loop(0, i, body, 0)
    block(i, True)

    o = acc_ref[...] / l_ref[...]
    lp = lam_ref[...]
    lam = (jnp.exp(jnp.sum(lp[0:1] * lp[1:2], axis=-1, keepdims=True))
           - jnp.exp(jnp.sum(lp[2:3] * lp[3:4], axis=-1, keepdims=True)) + lam_init)
    d = o[:tq] - lam * o[tq:]
    ms = jnp.mean(d * d, axis=-1, keepdims=True)
    o_ref[...] = (d * lax.rsqrt(ms + EPS) * g_ref[...] * (1.0 - lam_init)).astype(o_ref.dtype)


def _diff_attn(qz, kr, proj, lam_pack, g, *, batch, seq, tq, lam_init):
    t = batch * seq
    nq = seq // tq
    return pl.pallas_call(
        functools.partial(_diff_attn_kernel, tq=tq, lam_init=lam_init),
        grid=(batch, DA_HEADS, nq),
        in_specs=[
            pl.BlockSpec((2, tq, LANES), lambda b, h, i: (0, b * nq + i, h)),
            pl.BlockSpec((seq, LANES), lambda b, h, i: (b, h)),
            pl.BlockSpec((seq, LANES), lambda b, h, i: (b, COL_DV // LANES + h)),
            pl.BlockSpec((8, LANES), lambda b, h, i: (0, 0)),
            pl.BlockSpec((1, LANES), lambda b, h, i: (0, 0)),
        ],
        out_specs=pl.BlockSpec((tq, LANES), lambda b, h, i: (b * nq + i, h)),
        out_shape=jax.ShapeDtypeStruct((t, HW), BF16),
        scratch_shapes=[
            pltpu.VMEM((2 * tq, 1), F32),
            pltpu.VMEM((2 * tq, 1), F32),
            pltpu.VMEM((2 * tq, LANES), F32),
        ],
        compiler_params=_cparams(("parallel", "parallel", "arbitrary")),
    )(qz, kr, proj, lam_pack, g)


def _unit_lower_inverse(a, ri, ci):
    eye = jnp.where(ri == ci, 1.0, 0.0)
    same_blk = (ri // 16) == (ci // 16)
    ad = jnp.where(same_blk, a, 0.0)
    ao = a - ad
    x = eye - ad
    p = _dot(ad, ad)
    x = x + _dot(x, p)
    p = _dot(p, p)
    x = x + _dot(x, p)
    p = _dot(p, p)
    x = x + _dot(x, p)
    bm = _dot(x, ao)
    b2 = _dot(bm, bm)
    pm = eye - bm + b2 - _dot(bm, b2)
    return _dot(pm, x)


def _gdn_prep_kernel(q_ref, k_ref, v_ref, qh_ref, kh_ref, vh_ref, ab_ref, wc_ref, alog_ref, dtb_ref,
                     pq_ref, c_ref, cd_ref, op_ref,
                     xp_ref, qn_ref, kn_ref, vn_ref, beta_ref, g_ref, grow_ref, *, ts):
    first = pl.program_id(1) == 0
    nc = ts // CHUNK

    for s, (cur_ref, halo_ref, dst_ref) in enumerate(
            ((q_ref, qh_ref, qn_ref), (k_ref, kh_ref, kn_ref), (v_ref, vh_ref, vn_ref))):
        halo = halo_ref[...].astype(F32)
        xp_ref[0:8, :] = jnp.where(first, 0.0, halo)
        xp_ref[8:, :] = cur_ref[...].astype(F32)
        acc = xp_ref[8:8 + ts, :] * wc_ref[CONV_K - 1:CONV_K, s * HW:(s + 1) * HW]
        for j in range(1, CONV_K):
            acc = acc + xp_ref[8 - j:8 - j + ts, :] * wc_ref[CONV_K - 1 - j:CONV_K - j, s * HW:(s + 1) * HW]
        act = acc * jax.nn.sigmoid(acc)
        if s < 2:
            for h in range(GDN_HEADS):
                sl = slice(h * LANES, (h + 1) * LANES)
                a_h = act[:, sl]
                dst_ref[:, sl] = a_h * lax.rsqrt(jnp.sum(a_h * a_h, axis=-1, keepdims=True) + EPS)
        else:
            dst_ref[...] = act

    ab = ab_ref[...]
    beta_ref[...] = jax.nn.sigmoid(ab)
    xa = ab + dtb_ref[...]
    softplus = jnp.maximum(xa, 0.0) + jnp.log1p(jnp.exp(-jnp.abs(xa)))
    la = -jnp.exp(alog_ref[...]) * softplus
    ti = lax.broadcasted_iota(jnp.int32, (ts, ts), 0)
    tj = lax.broadcasted_iota(jnp.int32, (ts, ts), 1)
    ltri = jnp.where((tj <= ti) & ((ti // CHUNK) == (tj // CHUNK)), 1.0, 0.0)
    g_all = jnp.dot(ltri, la, precision=lax.Precision.HIGHEST, preferred_element_type=F32)
    g_ref[...] = g_all
    sel_r = lax.broadcasted_iota(jnp.int32, (8, LANES), 0)
    sel_c = lax.broadcasted_iota(jnp.int32, (8, LANES), 1)
    sel = jnp.where(sel_c == sel_r + AB_A0, 1.0, 0.0)
    grow_all = lax.dot_general(sel, g_all, (((1,), (1,)), ((), ())),
                               precision=lax.Precision.HIGHEST, preferred_element_type=F32)
    for c in range(nc):
        grow_ref[c] = grow_all[:, c * CHUNK:(c + 1) * CHUNK]

    ri = lax.broadcasted_iota(jnp.int32, (CHUNK, CHUNK), 0)
    ci = lax.broadcasted_iota(jnp.int32, (CHUNK, CHUNK), 1)
    incl = ci <= ri
    strict = ci < ri
    scale = GDN_K_DIM ** -0.5

    def chunk_body(c, carry):
        r0 = pl.multiple_of(c * CHUNK, CHUNK)
        rows = pl.ds(r0, CHUNK)
        for h in range(GDN_HEADS):
            sl = slice(h * LANES, (h + 1) * LANES)
            qh = qn_ref[rows, sl] * scale
            kh = kn_ref[rows, sl]
            vh = vn_ref[rows, sl]
            gcol = g_ref[rows, AB_A0 + h:AB_A0 + h + 1]
            bcol = beta_ref[rows, AB_B0 + h:AB_B0 + h + 1]
            glast = g_ref[pl.ds(r0 + CHUNK - 1, 1), AB_A0 + h:AB_A0 + h + 1]
            grow = grow_ref[c, h:h + 1, :]
            egcol = jnp.exp(gcol)
            decay = jnp.where(incl, jnp.exp(jnp.where(incl, gcol - grow, 0.0)), 0.0)

            kb = kh.astype(BF16)
            qk_kk = _nt_dot(jnp.concatenate([qh.astype(BF16), kb], axis=0), kb)
            qk = qk_kk[:CHUNK]
            kk = qk_kk[CHUNK:]
            a = jnp.where(strict, bcol * kk * decay, 0.0)
            tinv = _unit_lower_inverse(a, ri, ci)

            rhs = jnp.concatenate([kh * (bcol * egcol), vh * bcol], axis=1)
            wu = _dot(tinv.astype(BF16), rhs.astype(BF16)).astype(BF16)
            qkm = (qk * decay).astype(BF16)
            r2 = _dot(qkm, wu)
            qp = qh * egcol - r2[:, :LANES]
            kd = (kh * jnp.exp(glast - gcol)).astype(BF16)
            r3 = _tn_dot(kd, wu)

            pq_ref[c, h, 0:LANES, :] = r3[:, :LANES].astype(pq_ref.dtype)
            pq_ref[c, h, LANES:, :] = qp.astype(pq_ref.dtype)
            c_ref[c, h] = r3[:, LANES:]
            cd_ref[c, h] = jnp.broadcast_to(jnp.exp(glast), (8, LANES))
            op_ref[rows, sl] = r2[:, LANES:]
        return carry

    lax.fori_loop(0, nc, chunk_body, 0)


def _gdn_prep(proj, ab, wconv, alog_row, dtb_row, *, batch, seq, ts):
    t = batch * seq
    nt = seq // ts
    nc = ts // CHUNK
    n_chunks = t // CHUNK
    hb = ts // 8

    def cur(col):
        return pl.BlockSpec((ts, HW), lambda b, i: (b * nt + i, col // HW))

    def halo(col):
        return pl.BlockSpec((8, HW), lambda b, i: (jnp.maximum((b * nt + i) * hb - 1, 0), col // HW))

    return pl.pallas_call(
        functools.partial(_gdn_prep_kernel, ts=ts),
        grid=(batch, nt),
        in_specs=[
            cur(COL_GQ), cur(COL_GK), cur(COL_GV), halo(COL_GQ), halo(COL_GK), halo(COL_GV),
            pl.BlockSpec((ts, LANES), lambda b, i: (b * nt + i, 0)),
            pl.BlockSpec((CONV_K, 3 * HW), lambda b, i: (0, 0)),
            pl.BlockSpec((1, LANES), lambda b, i: (0, 0)),
            pl.BlockSpec((1, LANES), lambda b, i: (0, 0)),
        ],
        out_specs=[
            pl.BlockSpec((nc, GDN_HEADS, GDN_K_DIM + CHUNK, LANES), lambda b, i: (b * nt + i, 0, 0, 0)),
            pl.BlockSpec((nc, GDN_HEADS, GDN_K_DIM, LANES), lambda b, i: (b * nt + i, 0, 0, 0)),
            pl.BlockSpec((nc, GDN_HEADS, 8, LANES), lambda b, i: (b * nt + i, 0, 0, 0)),
            pl.BlockSpec((ts, HW), lambda b, i: (b * nt + i, 0)),
        ],
        out_shape=[
            jax.ShapeDtypeStruct((n_chunks, GDN_HEADS, GDN_K_DIM + CHUNK, LANES), BF16),
            jax.ShapeDtypeStruct((n_chunks, GDN_HEADS, GDN_K_DIM, LANES), F32),
            jax.ShapeDtypeStruct((n_chunks, GDN_HEADS, 8, LANES), F32),
            jax.ShapeDtypeStruct((t, HW), F32),
        ],
        scratch_shapes=[
            pltpu.VMEM((ts + 8, HW), F32),
            pltpu.VMEM((ts, HW), F32),
            pltpu.VMEM((ts, HW), F32),
            pltpu.VMEM((ts, HW), F32),
            pltpu.VMEM((ts, LANES), F32),
            pltpu.VMEM((ts, LANES), F32),
            pltpu.VMEM((nc, 8, CHUNK), F32),
        ],
        compiler_params=_cparams(("parallel", "parallel")),
    )(proj, proj, proj, proj, proj, proj, ab, wconv, alog_row, dtb_row)


def _gdn_scan_kernel(pq_ref, c_ref, cd_ref, op_ref, z_ref, gn_ref, o_ref, s_ref, *, ts):
    @pl.when(pl.program_id(1) == 0)
    def _():
        s_ref[...] = jnp.zeros(s_ref.shape, F32)

    nc = ts // CHUNK

    def chunk_body(c, carry):
        r0 = pl.multiple_of(c * CHUNK, CHUNK)
        rows = pl.ds(r0, CHUNK)
        for h in range(GDN_HEADS):
            sl = slice(h * LANES, (h + 1) * LANES)
            s = s_ref[h]
            r = _dot(pq_ref[c, h], s.astype(BF16))
            o = r[GDN_K_DIM:] + op_ref[rows, sl]
            s_ref[h] = cd_ref[c, h, 0:1, :] * s - r[:GDN_K_DIM] + c_ref[c, h]
            on = o * lax.rsqrt(jnp.mean(o * o, axis=-1, keepdims=True) + EPS) * gn_ref[...]
            z = z_ref[rows, sl].astype(F32)
            o_ref[rows, sl] = (on * (z * jax.nn.sigmoid(z))).astype(o_ref.dtype)
        return carry

    lax.fori_loop(0, nc, chunk_body, 0)


def _gdn_scan(pq, cmat, cd, op, proj, gn, *, batch, seq, ts):
    t = batch * seq
    nt = seq // ts
    nc = ts // CHUNK
    return pl.pallas_call(
        functools.partial(_gdn_scan_kernel, ts=ts),
        grid=(batch, nt),
        in_specs=[
            pl.BlockSpec((nc, GDN_HEADS, GDN_K_DIM + CHUNK, LANES), lambda b, i: (b * nt + i, 0, 0, 0)),
            pl.BlockSpec((nc, GDN_HEADS, GDN_K_DIM, LANES), lambda b, i: (b * nt + i, 0, 0, 0)),
            pl.BlockSpec((nc, GDN_HEADS, 8, LANES), lambda b, i: (b * nt + i, 0, 0, 0)),
            pl.BlockSpec((ts, HW), lambda b, i: (b * nt + i, 0)),
            pl.BlockSpec((ts, HW), lambda b, i: (b * nt + i, COL_GZ // HW)),
            pl.BlockSpec((1, LANES), lambda b, i: (0, 0)),
        ],
        out_specs=pl.BlockSpec((ts, HW), lambda b, i: (b * nt + i, 0)),
        out_shape=jax.ShapeDtypeStruct((t, HW), BF16),
        scratch_shapes=[pltpu.VMEM((GDN_HEADS, GDN_K_DIM, GDN_V_DIM), F32)],
        compiler_params=_cparams(("parallel", "arbitrary")),
    )(pq, cmat, cd, op, proj, gn)


def _mem_kv_kernel(m_ref, g_ref, w_ref, gk_ref, k_ref, v_ref):
    m = m_ref[...]
    mn = m * lax.rsqrt(jnp.mean(m * m, axis=-1, keepdims=True) + EPS) * g_ref[...]
    kv = _dot(mn.astype(BF16), w_ref[...])
    for h in range(CA_HEADS):
        sl = slice(h * LANES, (h + 1) * LANES)
        kh = kv[:, sl]
        k_ref[:, sl] = (kh * lax.rsqrt(jnp.mean(kh * kh, axis=-1, keepdims=True) + EPS)
                        * gk_ref[...]).astype(k_ref.dtype)
    v_ref[...] = kv[:, CA_Q:].astype(v_ref.dtype)


def _mem_kv(mem2, g, w, gk, *, batch, n_mem):
    d = mem2.shape[1]
    return pl.pallas_call(
        _mem_kv_kernel,
        grid=(batch,),
        in_specs=[
            pl.BlockSpec((n_mem, d), lambda b: (b, 0)),
            pl.BlockSpec((1, d), lambda b: (0, 0)),
            pl.BlockSpec((d, 2 * CA_Q), lambda b: (0, 0)),
            pl.BlockSpec((1, LANES), lambda b: (0, 0)),
        ],
        out_specs=[
            pl.BlockSpec((n_mem, CA_Q), lambda b: (b, 0)),
            pl.BlockSpec((n_mem, CA_Q), lambda b: (b, 0)),
        ],
        out_shape=[
            jax.ShapeDtypeStruct((batch * n_mem, CA_Q), BF16),
            jax.ShapeDtypeStruct((batch * n_mem, CA_Q), BF16),
        ],
        compiler_params=_cparams(("parallel",)),
    )(mem2, g, w, gk)


def _cross_attn_kernel(q_ref, k_ref, v_ref, g_ref, o_ref):
    q = q_ref[...].astype(F32)
    qn = q * lax.rsqrt(jnp.mean(q * q, axis=-1, keepdims=True) + EPS) * g_ref[...] * (CA_DIM ** -0.5)
    s = _nt_dot(qn.astype(BF16), k_ref[...])
    p = jnp.exp(s - jnp.max(s, axis=-1, keepdims=True))
    l = jnp.sum(p, axis=-1, keepdims=True)
    o_ref[...] = (_dot(p.astype(BF16), v_ref[...]) / l).astype(o_ref.dtype)


def _cross_attn(proj, kc, vc, g, *, batch, seq, n_mem, tq):
    t = batch * seq
    nq = seq // tq
    return pl.pallas_call(
        _cross_attn_kernel,
        grid=(batch, nq, CA_HEADS),
        in_specs=[
            pl.BlockSpec((tq, LANES), lambda b, i, h: (b * nq + i, COL_CQ // LANES + h)),
            pl.BlockSpec((n_mem, LANES), lambda b, i, h: (b, h)),
            pl.BlockSpec((n_mem, LANES), lambda b, i, h: (b, h)),
            pl.BlockSpec((1, LANES), lambda b, i, h: (0, 0)),
        ],
        out_specs=pl.BlockSpec((tq, LANES), lambda b, i, h: (b * nq + i, h)),
        out_shape=jax.ShapeDtypeStruct((t, CA_Q), BF16),
        compiler_params=_cparams(("parallel", "parallel", "parallel")),
    )(proj, kc, vc, g)


def _merge_kernel(x_ref, oa_ref, ob_ref, oc_ref, gt_ref, wa_ref, wb_ref, wc_ref, wo_ref, o_ref):
    d = x_ref.shape[1]
    ya = _dot(oa_ref[...], wa_ref[...])
    yb = _dot(ob_ref[...], wb_ref[...])
    yc = _dot(oc_ref[...], wc_ref[...])
    mixed = (gt_ref[:, 0:d].astype(F32) * ya + gt_ref[:, d:2 * d].astype(F32) * yb
             + gt_ref[:, 2 * d:3 * d].astype(F32) * yc)
    o_ref[...] = x_ref[...] + _dot(mixed.astype(BF16), wo_ref[...])


def _merge(x, oa, ob, oc, gates, wa, wb, wc, wo, *, tm):
    t, d = x.shape

    def tok(n):
        return pl.BlockSpec((tm, n), lambda i: (i, 0))

    def full(a):
        return pl.BlockSpec(a.shape, lambda i: (0, 0))

    return pl.pallas_call(
        _merge_kernel,
        grid=(t // tm,),
        in_specs=[tok(d), tok(HW), tok(HW), tok(CA_Q), tok(N_BRANCH * d),
                  full(wa), full(wb), full(wc), full(wo)],
        out_specs=tok(d),
        out_shape=jax.ShapeDtypeStruct((t, d), F32),
        compiler_params=_cparams(("parallel",)),
    )(x, oa, ob, oc, gates, wa, wb, wc, wo)


def _mlp_kernel(x_ref, g_ref, w1_ref, w2_ref, o_ref, h_ref, acc_ref):
    k = pl.program_id(1)

    @pl.when(k == 0)
    def _():
        x = x_ref[...]
        ms = jnp.mean(x * x, axis=-1, keepdims=True)
        h_ref[...] = (x * lax.rsqrt(ms + EPS) * g_ref[...]).astype(h_ref.dtype)
        acc_ref[...] = x

    a = jnp.maximum(_dot(h_ref[...], w1_ref[...]), 0.0)
    acc_ref[...] += _dot((a * a).astype(BF16), w2_ref[...])

    @pl.when(k == pl.num_programs(1) - 1)
    def _():
        o_ref[...] = acc_ref[...]


def _mlp(x, g, w1, w2, *, tm, tf):
    t, d = x.shape
    f = w1.shape[1]
    return pl.pallas_call(
        _mlp_kernel,
        grid=(t // tm, f // tf),
        in_specs=[
            pl.BlockSpec((tm, d), lambda i, k: (i, 0)),
            pl.BlockSpec((1, d), lambda i, k: (0, 0)),
            pl.BlockSpec((d, tf), lambda i, k: (0, k)),
            pl.BlockSpec((tf, d), lambda i, k: (k, 0)),
        ],
        out_specs=pl.BlockSpec((tm, d), lambda i, k: (i, 0)),
        out_shape=jax.ShapeDtypeStruct((t, d), F32),
        scratch_shapes=[pltpu.VMEM((tm, d), BF16), pltpu.VMEM((tm, d), F32)],
        compiler_params=_cparams(("parallel", "arbitrary")),
    )(x, g, w1, w2)


def _pick(n, pref):
    t = min(pref, n)
    while n % t:
        t //= 2
    return t


def _row(v, width=None):
    v = v.astype(F32).reshape(1, -1)
    if width is not None and v.shape[1] < width:
        v = jnp.pad(v, ((0, 0), (0, width - v.shape[1])))
    return v


def _layer(x2, mem2, pos2, inv_row, batch, seq, n_mem, l, p):
    t, d = x2.shape
    (ln1_g, w_in, w_gate, b_gate, da_qnorm_g, da_knorm_g, lq1, lk1, lq2, lk2, da_subln_g, w_o_diff,
     w_conv, gdn_a_log, gdn_dt_bias, gdn_norm_g, w_o_delta, mem_norm_g, w_mem_kv, ca_qnorm_g,
     ca_knorm_g, w_o_cross, w_out, ln2_g, w_mlp1, w_mlp2) = p

    o = [0]
    for n in (HW, HW, HW, HW, HW, HW, HW, GDN_HEADS, GDN_HEADS, CA_Q):
        o.append(o[-1] + n)
    sec = [w_in[:, o[i]:o[i + 1]] for i in range(10)]
    w_proj = jnp.concatenate([sec[3], sec[4], sec[5], sec[6], sec[0], sec[1], sec[2], sec[9]],
                             axis=1).astype(BF16)
    w_ab = jnp.zeros((d, LANES), F32)
    w_ab = w_ab.at[:, AB_B0:AB_B0 + GDN_HEADS].set(sec[7]).at[:, AB_A0:AB_A0 + GDN_HEADS].set(sec[8])
    w_ab = w_ab.astype(BF16)
    zeros_proj = jnp.zeros((1, N_PROJ), F32)
    zeros_ab = jnp.zeros((1, LANES), F32)
    g1 = _row(ln1_g)

    tm = _pick(t, 1024)
    proj = _norm_matmul(x2, g1, w_proj, zeros_proj, act=None, out_dtype=BF16, tm=_pick(t, 512),
                        tn=N_PROJ // 2)
    ab = _norm_matmul(x2, g1, w_ab, zeros_ab, act=None, out_dtype=F32, tm=tm, tn=LANES)
    gates = _norm_matmul(x2, g1, w_gate.astype(BF16), _row(b_gate), act="sigmoid", out_dtype=BF16,
                         tm=tm, tn=1024)

    gq = _row(jnp.tile(da_qnorm_g, 2))
    gk = _row(jnp.tile(da_knorm_g, 2))
    qz, kr = _qk_prep(proj, pos2, inv_row, gq, gk, tm=_pick(t, 512))
    lam_pack = jnp.zeros((8, LANES), F32)
    lam_pack = lam_pack.at[0:4, 0:DA_QK_DIM].set(jnp.stack([lq1, lk1, lq2, lk2]).astype(F32))
    lam_init = 0.8 - 0.6 * math.exp(-0.3 * l)
    oa = _diff_attn(qz, kr, proj, lam_pack, _row(da_subln_g), batch=batch, seq=seq,
                    tq=_pick(seq, 512), lam_init=lam_init)

    alog_row = jnp.zeros((1, LANES), F32).at[0, AB_A0:AB_A0 + GDN_HEADS].set(gdn_a_log.astype(F32))
    dtb_row = jnp.zeros((1, LANES), F32).at[0, AB_A0:AB_A0 + GDN_HEADS].set(gdn_dt_bias.astype(F32))
    pq, cmat, cd, op = _gdn_prep(proj, ab, w_conv.astype(F32), alog_row, dtb_row,
                                 batch=batch, seq=seq, ts=_pick(seq, 256))
    ob = _gdn_scan(pq, cmat, cd, op, proj, _row(gdn_norm_g), batch=batch, seq=seq, ts=_pick(seq, 512))

    kc, vc = _mem_kv(mem2, _row(mem_norm_g), w_mem_kv.astype(BF16), _row(ca_knorm_g),
                     batch=batch, n_mem=n_mem)
    oc = _cross_attn(proj, kc, vc, _row(ca_qnorm_g), batch=batch, seq=seq, n_mem=n_mem,
                     tq=_pick(seq, 1024))

    x2 = _merge(x2, oa, ob, oc, gates, w_o_diff.astype(BF16), w_o_delta.astype(BF16),
                w_o_cross.astype(BF16), w_out.astype(BF16), tm=_pick(t, 512))
    x2 = _mlp(x2, _row(ln2_g), w_mlp1.astype(BF16), w_mlp2.astype(BF16), tm=_pick(t, 1024), tf=1024)
    return x2


def kernel(x, mem, positions, ln1_g, w_in, w_gate, b_gate, da_qnorm_g, da_knorm_g, da_lambda_q1, da_lambda_k1, da_lambda_q2, da_lambda_k2, da_subln_g, w_o_diff, w_conv, gdn_a_log, gdn_dt_bias, gdn_norm_g, w_o_delta, mem_norm_g, w_mem_kv, ca_qnorm_g, ca_knorm_g, w_o_cross, w_out, ln2_g, w_mlp1, w_mlp2):
    batch, seq, d = x.shape
    n_mem = mem.shape[1]
    x2 = x.reshape(batch * seq, d)
    mem2 = mem.reshape(batch * n_mem, d)
    pos2 = positions.reshape(batch * seq, 1)
    half = DA_QK_DIM // 2
    inv = jnp.exp(-math.log(ROPE_THETA) * jnp.arange(half, dtype=F32) / half)
    inv_row = jnp.tile(inv, LANES // half).reshape(1, LANES)
    params = (ln1_g, w_in, w_gate, b_gate, da_qnorm_g, da_knorm_g, da_lambda_q1, da_lambda_k1,
              da_lambda_q2, da_lambda_k2, da_subln_g, w_o_diff, w_conv, gdn_a_log, gdn_dt_bias,
              gdn_norm_g, w_o_delta, mem_norm_g, w_mem_kv, ca_qnorm_g, ca_knorm_g, w_o_cross, w_out,
              ln2_g, w_mlp1, w_mlp2)
    for l in range(ln1_g.shape[0]):
        x2 = _layer(x2, mem2, pos2, inv_row, batch, seq, n_mem, l, tuple(a[l] for a in params))
    return x2.reshape(batch, seq, d)
```

```python
import functools
import math

import jax
import jax.numpy as jnp
from jax import lax
from jax.experimental import pallas as pl
from jax.experimental.pallas import tpu as pltpu

F32 = jnp.float32
BF16 = jnp.bfloat16

EPS = 1e-6
CHUNK = 64
ROPE_THETA = 10000.0
N_BRANCH = 3

DA_HEADS = 6
DA_QK_DIM = 64
DA_V_DIM = 128
GDN_HEADS = 6
GDN_K_DIM = 128
GDN_V_DIM = 128
CONV_K = 4
CA_HEADS = 4
CA_DIM = 128

LANES = 128
HW = GDN_HEADS * GDN_K_DIM
CA_Q = CA_HEADS * CA_DIM

COL_GQ, COL_GK, COL_GV, COL_GZ, COL_DQ, COL_DK, COL_DV, COL_CQ = (
    0, HW, 2 * HW, 3 * HW, 4 * HW, 5 * HW, 6 * HW, 7 * HW)
N_PROJ = 7 * HW + CA_Q
AB_B0, AB_A0 = 0, 8

VMEM_LIMIT = 48 * 1024 * 1024


def _cparams(sem):
    return pltpu.CompilerParams(dimension_semantics=sem, vmem_limit_bytes=VMEM_LIMIT)


def _nt_dot(a, b):
    return lax.dot_general(a, b, (((1,), (1,)), ((), ())), preferred_element_type=F32)


def _tn_dot(a, b):
    return lax.dot_general(a, b, (((0,), (0,)), ((), ())), preferred_element_type=F32)


def _dot(a, b):
    return jnp.dot(a, b, preferred_element_type=F32)


def _norm_matmul_kernel(x_ref, g_ref, w_ref, b_ref, o_ref, h_ref, *, act):
    @pl.when(pl.program_id(1) == 0)
    def _():
        x = x_ref[...]
        ms = jnp.mean(x * x, axis=-1, keepdims=True)
        h_ref[...] = (x * lax.rsqrt(ms + EPS) * g_ref[...]).astype(h_ref.dtype)

    y = _dot(h_ref[...], w_ref[...]) + b_ref[...]
    if act == "sigmoid":
        y = jax.nn.sigmoid(y)
    o_ref[...] = y.astype(o_ref.dtype)


def _norm_matmul(x, g, w, b, *, act, out_dtype, tm, tn, name):
    t, d = x.shape
    n = w.shape[1]
    return pl.pallas_call(
        functools.partial(_norm_matmul_kernel, act=act),
        grid=(t // tm, n // tn),
        in_specs=[
            pl.BlockSpec((tm, d), lambda i, j: (i, 0)),
            pl.BlockSpec((1, d), lambda i, j: (0, 0)),
            pl.BlockSpec((d, tn), lambda i, j: (0, j)),
            pl.BlockSpec((1, tn), lambda i, j: (0, j)),
        ],
        out_specs=pl.BlockSpec((tm, tn), lambda i, j: (i, j)),
        out_shape=jax.ShapeDtypeStruct((t, n), out_dtype),
        scratch_shapes=[pltpu.VMEM((tm, d), BF16)],
        compiler_params=_cparams(("parallel", "arbitrary")),
        name=name,
    )(x, g, w, b)


def _qk_prep_kernel(q_ref, k_ref, pos_ref, inv_ref, gq_ref, gk_ref, qz_ref, kr_ref):
    tm = q_ref.shape[0]
    ang = pos_ref[...].astype(F32) * inv_ref[...]
    cos = jnp.cos(ang)
    sin = jnp.sin(ang)
    lane = lax.broadcasted_iota(jnp.int32, (tm, LANES), 1)
    lo_map = lane < DA_QK_DIM
    first_half = (lane % DA_QK_DIM) < (DA_QK_DIM // 2)
    sin_signed = jnp.where(first_half, -sin, sin)
    scale = DA_QK_DIM ** -0.5 * math.log2(math.e)

    def norm_rope(x, g):
        s = x * x
        tot = jnp.sum(s, axis=-1, keepdims=True)
        lo = jnp.sum(jnp.where(lo_map, s, 0.0), axis=-1, keepdims=True)
        ms = jnp.where(lo_map, lo, tot - lo) * (1.0 / DA_QK_DIM)
        y = x * lax.rsqrt(ms + EPS) * g
        partner = jnp.where(first_half,
                            pltpu.roll(y, LANES - DA_QK_DIM // 2, axis=1),
                            pltpu.roll(y, DA_QK_DIM // 2, axis=1))
        return y * cos + partner * sin_signed

    for h in range(DA_HEADS):
        sl = slice(h * LANES, (h + 1) * LANES)
        q = norm_rope(q_ref[:, sl].astype(F32), gq_ref[...]) * scale
        qz_ref[0, :, sl] = jnp.where(lo_map, q, 0.0).astype(qz_ref.dtype)
        qz_ref[1, :, sl] = jnp.where(lo_map, 0.0, q).astype(qz_ref.dtype)
        k = norm_rope(k_ref[:, sl].astype(F32), gk_ref[...])
        kr_ref[:, sl] = k.astype(kr_ref.dtype)


def _qk_prep(proj, pos, inv, gq, gk, *, tm):
    t = proj.shape[0]
    return pl.pallas_call(
        _qk_prep_kernel,
        grid=(t // tm,),
        in_specs=[
            pl.BlockSpec((tm, HW), lambda i: (i, COL_DQ // HW)),
            pl.BlockSpec((tm, HW), lambda i: (i, COL_DK // HW)),
            pl.BlockSpec((tm, 1), lambda i: (i, 0)),
            pl.BlockSpec((1, LANES), lambda i: (0, 0)),
            pl.BlockSpec((1, LANES), lambda i: (0, 0)),
            pl.BlockSpec((1, LANES), lambda i: (0, 0)),
        ],
        out_specs=[
            pl.BlockSpec((2, tm, HW), lambda i: (0, i, 0)),
            pl.BlockSpec((tm, HW), lambda i: (i, 0)),
        ],
        out_shape=[
            jax.ShapeDtypeStruct((2, t, HW), BF16),
            jax.ShapeDtypeStruct((t, HW), BF16),
        ],
        compiler_params=_cparams(("parallel",)),
        name="qk_prep",
    )(proj, proj, pos, inv, gq, gk)


def _diff_attn_kernel(qz_ref, k_ref, v_ref, lam_ref, g_ref, o_ref, m_ref, l_ref, acc_ref,
                      *, tq, tk, sub, ahead, lam_init):
    i = pl.program_id(2)
    n_sub = 2 * tq // sub
    m_ref[...] = jnp.full(m_ref.shape, -jnp.inf, F32)
    l_ref[...] = jnp.zeros(l_ref.shape, F32)
    acc_ref[...] = jnp.zeros(acc_ref.shape, F32)

    def scores(base, item):
        r, k0, nk, _ = item
        mp, off = divmod(r * sub, tq)
        return _nt_dot(qz_ref[mp, off:off + sub, :], k_ref[pl.ds(base + k0, nk), :])

    def consume(base, item, s):
        r, k0, nk, masked = item
        off = (r * sub) % tq
        rs = slice(r * sub, (r + 1) * sub)
        if masked:
            row = lax.broadcasted_iota(jnp.int32, s.shape, 0) + off
            col = lax.broadcasted_iota(jnp.int32, s.shape, 1) + k0
            s = jnp.where((col // CHUNK) <= (row // CHUNK), s, -jnp.inf)
        m_prev = m_ref[rs]
        m_new = jnp.maximum(m_prev, jnp.max(s, axis=-1, keepdims=True))
        alpha = jnp.exp2(m_prev - m_new)
        p = jnp.exp2(s - pltpu.repeat(m_new, nk // LANES, axis=1))
        psum = p[:, 0:LANES]
        for c in range(1, nk // LANES):
            psum = psum + p[:, c * LANES:(c + 1) * LANES]
        l_ref[rs] = alpha * l_ref[rs] + psum
        v = v_ref[pl.ds(base + k0, nk), :]
        acc_ref[rs] = alpha * acc_ref[rs] + _dot(p.astype(v.dtype), v)
        m_ref[rs] = m_new

    def pipelined(base, items):
        pending = [scores(base, it) for it in items[:ahead]]
        for n, it in enumerate(items):
            s = pending.pop(0)
            if n + ahead < len(items):
                pending.append(scores(base, items[n + ahead]))
            consume(base, it, s)

    full_items = [(r, 0, tk, False) for r in range(n_sub)]

    def body(j, carry):
        pipelined(pl.multiple_of(j * tk, tk), full_items)
        return carry

    lax.fori_loop(0, i * (tq // tk), body, 0)

    diag_items = []
    for kb in range(tq // tk):
        for r in range(n_sub):
            off = (r * sub) % tq
            k0, k1 = kb * tk, min((kb + 1) * tk, off + sub)
            if k1 > k0:
                diag_items.append((r, k0, k1 - k0, k1 > off))
    pipelined(pl.multiple_of(i * tq, tq), diag_items)

    o = acc_ref[...] / jnp.sum(l_ref[...], axis=-1, keepdims=True)
    lp = lam_ref[...]
    lam = (jnp.exp(jnp.sum(lp[0:1] * lp[1:2], axis=-1, keepdims=True))
           - jnp.exp(jnp.sum(lp[2:3] * lp[3:4], axis=-1, keepdims=True)) + lam_init)
    d = o[:tq] - lam * o[tq:]
    ms = jnp.mean(d * d, axis=-1, keepdims=True)
    o_ref[...] = (d * lax.rsqrt(ms + EPS) * g_ref[...] * (1.0 - lam_init)).astype(o_ref.dtype)


def _diff_attn(qz, kr, proj, lam_pack, g, *, batch, seq, tq, tk, sub, ahead, lam_init):
    t = batch * seq
    nq = seq // tq
    return pl.pallas_call(
        functools.partial(_diff_attn_kernel, tq=tq, tk=tk, sub=sub, ahead=ahead, lam_init=lam_init),
        grid=(batch, DA_HEADS, nq),
        in_specs=[
            pl.BlockSpec((2, tq, LANES), lambda b, h, i: (0, b * nq + i, h)),
            pl.BlockSpec((seq, LANES), lambda b, h, i: (b, h)),
            pl.BlockSpec((seq, LANES), lambda b, h, i: (b, COL_DV // LANES + h)),
            pl.BlockSpec((8, LANES), lambda b, h, i: (0, 0)),
            pl.BlockSpec((1, LANES), lambda b, h, i: (0, 0)),
        ],
        out_specs=pl.BlockSpec((tq, LANES), lambda b, h, i: (b * nq + i, h)),
        out_shape=jax.ShapeDtypeStruct((t, HW), BF16),
        scratch_shapes=[
            pltpu.VMEM((2 * tq, LANES), F32),
            pltpu.VMEM((2 * tq, LANES), F32),
            pltpu.VMEM((2 * tq, LANES), F32),
        ],
        compiler_params=_cparams(("parallel", "parallel", "arbitrary")),
        name="diff_attn",
    )(qz, kr, proj, lam_pack, g)


def _gdn_prep_kernel(q_ref, k_ref, v_ref, qh_ref, kh_ref, vh_ref, ab_ref, wc_ref, alog_ref, dtb_ref,
                     pq_ref, c_ref, cd_ref, op_ref,
                     xp_ref, qn_ref, kn_ref, vn_ref, beta_ref, g_ref, grow_ref, *, ts, cpb):
    first = pl.program_id(1) == 0
    nc = ts // CHUNK

    for s, (cur_ref, halo_ref, dst_ref) in enumerate(
            ((q_ref, qh_ref, qn_ref), (k_ref, kh_ref, kn_ref), (v_ref, vh_ref, vn_ref))):
        halo = halo_ref[...].astype(F32)
        xp_ref[0:8, :] = jnp.where(first, 0.0, halo)
        xp_ref[8:, :] = cur_ref[...].astype(F32)
        acc = xp_ref[8:8 + ts, :] * wc_ref[CONV_K - 1:CONV_K, s * HW:(s + 1) * HW]
        for j in range(1, CONV_K):
            acc = acc + xp_ref[8 - j:8 - j + ts, :] * wc_ref[CONV_K - 1 - j:CONV_K - j, s * HW:(s + 1) * HW]
        act = acc * jax.nn.sigmoid(acc)
        if s < 2:
            for h in range(GDN_HEADS):
                sl = slice(h * LANES, (h + 1) * LANES)
                a_h = act[:, sl]
                dst_ref[:, sl] = a_h * lax.rsqrt(jnp.sum(a_h * a_h, axis=-1, keepdims=True) + EPS)
        else:
            dst_ref[...] = act

    ab = ab_ref[...]
    beta_ref[...] = jax.nn.sigmoid(ab)
    xa = ab + dtb_ref[...]
    softplus = jnp.maximum(xa, 0.0) + jnp.log1p(jnp.exp(-jnp.abs(xa)))
    la = -jnp.exp(alog_ref[...]) * softplus
    ti = lax.broadcasted_iota(jnp.int32, (ts, ts), 0)
    tj = lax.broadcasted_iota(jnp.int32, (ts, ts), 1)
    ltri = jnp.where((tj <= ti) & ((ti // CHUNK) == (tj // CHUNK)), 1.0, 0.0)
    g_all = jnp.dot(ltri, la, precision=lax.Precision.HIGHEST, preferred_element_type=F32)
    g_ref[...] = g_all
    sel_r = lax.broadcasted_iota(jnp.int32, (8, LANES), 0)
    sel_c = lax.broadcasted_iota(jnp.int32, (8, LANES), 1)
    sel = jnp.where(sel_c == sel_r + AB_A0, 1.0, 0.0)
    grow_all = lax.dot_general(sel, g_all, (((1,), (1,)), ((), ())),
                               precision=lax.Precision.HIGHEST, preferred_element_type=F32)
    for c in range(nc):
        grow_ref[c] = grow_all[:, c * CHUNK:(c + 1) * CHUNK]

    ri = lax.broadcasted_iota(jnp.int32, (CHUNK, CHUNK), 0)
    ci = lax.broadcasted_iota(jnp.int32, (CHUNK, CHUNK), 1)
    incl = ci <= ri
    strict = ci < ri
    scale = GDN_K_DIM ** -0.5

    eye = jnp.where(ri == ci, 1.0, 0.0)
    same_blk = (ri // 16) == (ci // 16)

    def bdot(a, b):
        return _dot(a.astype(BF16), b.astype(BF16))

    def group_body(cg, carry):
        us = []
        for cc in range(cpb):
            c = cg * cpb + cc
            r0 = pl.multiple_of(c * CHUNK, CHUNK)
            rows = pl.ds(r0, CHUNK)
            for h in range(GDN_HEADS):
                sl = slice(h * LANES, (h + 1) * LANES)
                gcol = g_ref[rows, AB_A0 + h:AB_A0 + h + 1]
                glast = g_ref[pl.ds(r0 + CHUNK - 1, 1), AB_A0 + h:AB_A0 + h + 1]
                grow = grow_ref[c, h:h + 1, :]
                us.append(dict(
                    c=c, h=h, rows=rows, sl=sl, gcol=gcol, glast=glast,
                    q=qn_ref[rows, sl] * scale, k=kn_ref[rows, sl], v=vn_ref[rows, sl],
                    b=beta_ref[rows, AB_B0 + h:AB_B0 + h + 1], eg=jnp.exp(gcol),
                    decay=jnp.where(incl, jnp.exp(jnp.where(incl, gcol - grow, 0.0)), 0.0)))

        for u in us:
            kb = u["k"].astype(BF16)
            u["qk_kk"] = _nt_dot(jnp.concatenate([u["q"].astype(BF16), kb], axis=0), kb)
        for u in us:
            a = jnp.where(strict, u["b"] * u["qk_kk"][CHUNK:] * u["decay"], 0.0)
            ad = jnp.where(same_blk, a, 0.0)
            u["ao"] = a - ad
            u["x"] = eye - ad
            u["ad"] = ad
        for u in us:
            u["p"] = bdot(u["ad"], u["ad"])
        for _ in range(2):
            for u in us:
                xp = bdot(u["x"], u["p"])
                u["p"] = bdot(u["p"], u["p"])
                u["x"] = u["x"] + xp
        for u in us:
            u["x"] = u["x"] + bdot(u["x"], u["p"])
        for u in us:
            rhs = jnp.concatenate([u["k"] * (u["b"] * u["eg"]), u["v"] * u["b"]], axis=1)
            u["bm"] = bdot(u["x"], u["ao"])
            u["z"] = bdot(u["x"], rhs)
        for u in us:
            u["b2"] = bdot(u["bm"], u["bm"])
        for u in us:
            u["z"] = u["z"] + bdot(u["b2"], u["z"])
        for u in us:
            u["wu"] = (u["z"] - bdot(u["bm"], u["z"])).astype(BF16)
        for u in us:
            qkm = (u["qk_kk"][:CHUNK] * u["decay"]).astype(BF16)
            kd = (u["k"] * jnp.exp(u["glast"] - u["gcol"])).astype(BF16)
            u["r2"] = _dot(qkm, u["wu"])
            u["r3"] = _tn_dot(kd, u["wu"])
        for u in us:
            c, h = u["c"], u["h"]
            pq_ref[c, h, 0:LANES, :] = u["r3"][:, :LANES].astype(pq_ref.dtype)
            pq_ref[c, h, LANES:, :] = (u["q"] * u["eg"] - u["r2"][:, :LANES]).astype(pq_ref.dtype)
            c_ref[c, h] = u["r3"][:, LANES:]
            cd_ref[c, h] = jnp.broadcast_to(jnp.exp(u["glast"]), (8, LANES))
            op_ref[u["rows"], u["sl"]] = u["r2"][:, LANES:]
        return carry

    lax.fori_loop(0, nc // cpb, group_body, 0)


def _gdn_prep(proj, ab, wconv, alog_row, dtb_row, *, batch, seq, ts, cpb):
    t = batch * seq
    nt = seq // ts
    nc = ts // CHUNK
    n_chunks = t // CHUNK
    hb = ts // 8

    def cur(col):
        return pl.BlockSpec((ts, HW), lambda b, i: (b * nt + i, col // HW))

    def halo(col):
        return pl.BlockSpec((8, HW), lambda b, i: (jnp.maximum((b * nt + i) * hb - 1, 0), col // HW))

    return pl.pallas_call(
        functools.partial(_gdn_prep_kernel, ts=ts, cpb=cpb),
        grid=(batch, nt),
        in_specs=[
            cur(COL_GQ), cur(COL_GK), cur(COL_GV), halo(COL_GQ), halo(COL_GK), halo(COL_GV),
            pl.BlockSpec((ts, LANES), lambda b, i: (b * nt + i, 0)),
            pl.BlockSpec((CONV_K, 3 * HW), lambda b, i: (0, 0)),
            pl.BlockSpec((1, LANES), lambda b, i: (0, 0)),
            pl.BlockSpec((1, LANES), lambda b, i: (0, 0)),
        ],
        out_specs=[
            pl.BlockSpec((nc, GDN_HEADS, GDN_K_DIM + CHUNK, LANES), lambda b, i: (b * nt + i, 0, 0, 0)),
            pl.BlockSpec((nc, GDN_HEADS, GDN_K_DIM, LANES), lambda b, i: (b * nt + i, 0, 0, 0)),
            pl.BlockSpec((nc, GDN_HEADS, 8, LANES), lambda b, i: (b * nt + i, 0, 0, 0)),
            pl.BlockSpec((ts, HW), lambda b, i: (b * nt + i, 0)),
        ],
        out_shape=[
            jax.ShapeDtypeStruct((n_chunks, GDN_HEADS, GDN_K_DIM + CHUNK, LANES), BF16),
            jax.ShapeDtypeStruct((n_chunks, GDN_HEADS, GDN_K_DIM, LANES), F32),
            jax.ShapeDtypeStruct((n_chunks, GDN_HEADS, 8, LANES), F32),
            jax.ShapeDtypeStruct((t, HW), F32),
        ],
        scratch_shapes=[
            pltpu.VMEM((ts + 8, HW), F32),
            pltpu.VMEM((ts, HW), F32),
            pltpu.VMEM((ts, HW), F32),
            pltpu.VMEM((ts, HW), F32),
            pltpu.VMEM((ts, LANES), F32),
            pltpu.VMEM((ts, LANES), F32),
            pltpu.VMEM((nc, 8, CHUNK), F32),
        ],
        compiler_params=_cparams(("parallel", "parallel")),
        name="gdn_prep",
    )(proj, proj, proj, proj, proj, proj, ab, wconv, alog_row, dtb_row)


def _gdn_scan_kernel(pq_ref, c_ref, cd_ref, op_ref, z_ref, gn_ref, o_ref, s_ref, *, ts):
    @pl.when(pl.program_id(1) == 0)
    def _():
        s_ref[...] = jnp.zeros(s_ref.shape, F32)

    nc = ts // CHUNK

    def chunk_body(c, carry):
        r0 = pl.multiple_of(c * CHUNK, CHUNK)
        rows = pl.ds(r0, CHUNK)
        for h in range(GDN_HEADS):
            sl = slice(h * LANES, (h + 1) * LANES)
            s = s_ref[h]
            r = _dot(pq_ref[c, h], s.astype(BF16))
            o = r[GDN_K_DIM:] + op_ref[rows, sl]
            s_ref[h] = cd_ref[c, h, 0:1, :] * s - r[:GDN_K_DIM] + c_ref[c, h]
            on = o * lax.rsqrt(jnp.mean(o * o, axis=-1, keepdims=True) + EPS) * gn_ref[...]
            z = z_ref[rows, sl].astype(F32)
            o_ref[rows, sl] = (on * (z * jax.nn.sigmoid(z))).astype(o_ref.dtype)
        return carry

    lax.fori_loop(0, nc, chunk_body, 0)


def _gdn_scan(pq, cmat, cd, op, proj, gn, *, batch, seq, ts):
    t = batch * seq
    nt = seq // ts
    nc = ts // CHUNK
    return pl.pallas_call(
        functools.partial(_gdn_scan_kernel, ts=ts),
        grid=(batch, nt),
        in_specs=[
            pl.BlockSpec((nc, GDN_HEADS, GDN_K_DIM + CHUNK, LANES), lambda b, i: (b * nt + i, 0, 0, 0)),
            pl.BlockSpec((nc, GDN_HEADS, GDN_K_DIM, LANES), lambda b, i: (b * nt + i, 0, 0, 0)),
            pl.BlockSpec((nc, GDN_HEADS, 8, LANES), lambda b, i: (b * nt + i, 0, 0, 0)),
            pl.BlockSpec((ts, HW), lambda b, i: (b * nt + i, 0)),
            pl.BlockSpec((ts, HW), lambda b, i: (b * nt + i, COL_GZ // HW)),
            pl.BlockSpec((1, LANES), lambda b, i: (0, 0)),
        ],
        out_specs=pl.BlockSpec((ts, HW), lambda b, i: (b * nt + i, 0)),
        out_shape=jax.ShapeDtypeStruct((t, HW), BF16),
        scratch_shapes=[pltpu.VMEM((GDN_HEADS, GDN_K_DIM, GDN_V_DIM), F32)],
        compiler_params=_cparams(("parallel", "arbitrary")),
        name="gdn_scan",
    )(pq, cmat, cd, op, proj, gn)


def _mem_kv_kernel(m_ref, g_ref, w_ref, gk_ref, k_ref, v_ref):
    m = m_ref[...]
    mn = m * lax.rsqrt(jnp.mean(m * m, axis=-1, keepdims=True) + EPS) * g_ref[...]
    kv = _dot(mn.astype(BF16), w_ref[...])
    for h in range(CA_HEADS):
        sl = slice(h * LANES, (h + 1) * LANES)
        kh = kv[:, sl]
        k_ref[:, sl] = (kh * lax.rsqrt(jnp.mean(kh * kh, axis=-1, keepdims=True) + EPS)
                        * gk_ref[...]).astype(k_ref.dtype)
    v_ref[...] = kv[:, CA_Q:].astype(v_ref.dtype)


def _mem_kv(mem2, g, w, gk, *, batch, n_mem):
    d = mem2.shape[1]
    return pl.pallas_call(
        _mem_kv_kernel,
        grid=(batch,),
        in_specs=[
            pl.BlockSpec((n_mem, d), lambda b: (b, 0)),
            pl.BlockSpec((1, d), lambda b: (0, 0)),
            pl.BlockSpec((d, 2 * CA_Q), lambda b: (0, 0)),
            pl.BlockSpec((1, LANES), lambda b: (0, 0)),
        ],
        out_specs=[
            pl.BlockSpec((n_mem, CA_Q), lambda b: (b, 0)),
            pl.BlockSpec((n_mem, CA_Q), lambda b: (b, 0)),
        ],
        out_shape=[
            jax.ShapeDtypeStruct((batch * n_mem, CA_Q), BF16),
            jax.ShapeDtypeStruct((batch * n_mem, CA_Q), BF16),
        ],
        compiler_params=_cparams(("parallel",)),
        name="mem_kv",
    )(mem2, g, w, gk)


def _cross_attn_kernel(q_ref, k_ref, v_ref, g_ref, o_ref):
    q = q_ref[...].astype(F32)
    qn = q * lax.rsqrt(jnp.mean(q * q, axis=-1, keepdims=True) + EPS) * g_ref[...] * (CA_DIM ** -0.5)
    s = _nt_dot(qn.astype(BF16), k_ref[...])
    p = jnp.exp(s - jnp.max(s, axis=-1, keepdims=True))
    l = jnp.sum(p, axis=-1, keepdims=True)
    o_ref[...] = (_dot(p.astype(BF16), v_ref[...]) / l).astype(o_ref.dtype)


def _cross_attn(proj, kc, vc, g, *, batch, seq, n_mem, tq):
    t = batch * seq
    nq = seq // tq
    return pl.pallas_call(
        _cross_attn_kernel,
        grid=(batch, nq, CA_HEADS),
        in_specs=[
            pl.BlockSpec((tq, LANES), lambda b, i, h: (b * nq + i, COL_CQ // LANES + h)),
            pl.BlockSpec((n_mem, LANES), lambda b, i, h: (b, h)),
            pl.BlockSpec((n_mem, LANES), lambda b, i, h: (b, h)),
            pl.BlockSpec((1, LANES), lambda b, i, h: (0, 0)),
        ],
        out_specs=pl.BlockSpec((tq, LANES), lambda b, i, h: (b * nq + i, h)),
        out_shape=jax.ShapeDtypeStruct((t, CA_Q), BF16),
        compiler_params=_cparams(("parallel", "parallel", "parallel")),
        name="cross_attn",
    )(proj, kc, vc, g)


def _merge_kernel(x_ref, oa_ref, ob_ref, oc_ref, gt_ref, wa_ref, wb_ref, wc_ref, wo_ref, o_ref):
    d = x_ref.shape[1]
    ya = _dot(oa_ref[...], wa_ref[...])
    yb = _dot(ob_ref[...], wb_ref[...])
    yc = _dot(oc_ref[...], wc_ref[...])
    mixed = (gt_ref[:, 0:d].astype(F32) * ya + gt_ref[:, d:2 * d].astype(F32) * yb
             + gt_ref[:, 2 * d:3 * d].astype(F32) * yc)
    o_ref[...] = x_ref[...] + _dot(mixed.astype(BF16), wo_ref[...])


def _merge(x, oa, ob, oc, gates, wa, wb, wc, wo, *, tm):
    t, d = x.shape

    def tok(n):
        return pl.BlockSpec((tm, n), lambda i: (i, 0))

    def full(a):
        return pl.BlockSpec(a.shape, lambda i: (0, 0))

    return pl.pallas_call(
        _merge_kernel,
        grid=(t // tm,),
        in_specs=[tok(d), tok(HW), tok(HW), tok(CA_Q), tok(N_BRANCH * d),
                  full(wa), full(wb), full(wc), full(wo)],
        out_specs=tok(d),
        out_shape=jax.ShapeDtypeStruct((t, d), F32),
        compiler_params=_cparams(("parallel",)),
        name="merge",
    )(x, oa, ob, oc, gates, wa, wb, wc, wo)


def _mlp_kernel(x_ref, g_ref, w1_ref, w2_ref, o_ref, h_ref, acc_ref):
    k = pl.program_id(1)

    @pl.when(k == 0)
    def _():
        x = x_ref[...]
        ms = jnp.mean(x * x, axis=-1, keepdims=True)
        h_ref[...] = (x * lax.rsqrt(ms + EPS) * g_ref[...]).astype(h_ref.dtype)
        acc_ref[...] = x

    a = jnp.maximum(_dot(h_ref[...], w1_ref[...]), 0.0)
    acc_ref[...] += _dot((a * a).astype(BF16), w2_ref[...])

    @pl.when(k == pl.num_programs(1) - 1)
    def _():
        o_ref[...] = acc_ref[...]


def _mlp(x, g, w1, w2, *, tm, tf):
    t, d = x.shape
    f = w1.shape[1]
    return pl.pallas_call(
        _mlp_kernel,
        grid=(t // tm, f // tf),
        in_specs=[
            pl.BlockSpec((tm, d), lambda i, k: (i, 0)),
            pl.BlockSpec((1, d), lambda i, k: (0, 0)),
            pl.BlockSpec((d, tf), lambda i, k: (0, k)),
            pl.BlockSpec((tf, d), lambda i, k: (k, 0)),
        ],
        out_specs=pl.BlockSpec((tm, d), lambda i, k: (i, 0)),
        out_shape=jax.ShapeDtypeStruct((t, d), F32),
        scratch_shapes=[pltpu.VMEM((tm, d), BF16), pltpu.VMEM((tm, d), F32)],
        compiler_params=_cparams(("parallel", "arbitrary")),
        name="mlp",
    )(x, g, w1, w2)


def _pick(n, pref):
    t = min(pref, n)
    while n % t:
        t //= 2
    return t


def _row(v, width=None):
    v = v.astype(F32).reshape(1, -1)
    if width is not None and v.shape[1] < width:
        v = jnp.pad(v, ((0, 0), (0, width - v.shape[1])))
    return v


def _layer(x2, mem2, pos2, inv_row, batch, seq, n_mem, l, p):
    t, d = x2.shape
    (ln1_g, w_in, w_gate, b_gate, da_qnorm_g, da_knorm_g, lq1, lk1, lq2, lk2, da_subln_g, w_o_diff,
     w_conv, gdn_a_log, gdn_dt_bias, gdn_norm_g, w_o_delta, mem_norm_g, w_mem_kv, ca_qnorm_g,
     ca_knorm_g, w_o_cross, w_out, ln2_g, w_mlp1, w_mlp2) = p

    o = [0]
    for n in (HW, HW, HW, HW, HW, HW, HW, GDN_HEADS, GDN_HEADS, CA_Q):
        o.append(o[-1] + n)
    sec = [w_in[:, o[i]:o[i + 1]] for i in range(10)]
    w_proj = jnp.concatenate([sec[3], sec[4], sec[5], sec[6], sec[0], sec[1], sec[2], sec[9]],
                             axis=1).astype(BF16)
    w_ab = jnp.zeros((d, LANES), F32)
    w_ab = w_ab.at[:, AB_B0:AB_B0 + GDN_HEADS].set(sec[7]).at[:, AB_A0:AB_A0 + GDN_HEADS].set(sec[8])
    w_ab = w_ab.astype(BF16)
    zeros_proj = jnp.zeros((1, N_PROJ), F32)
    zeros_ab = jnp.zeros((1, LANES), F32)
    g1 = _row(ln1_g)

    tm = _pick(t, 1024)
    proj = _norm_matmul(x2, g1, w_proj, zeros_proj, act=None, out_dtype=BF16, tm=_pick(t, 512),
                        tn=N_PROJ // 2, name="proj")
    ab = _norm_matmul(x2, g1, w_ab, zeros_ab, act=None, out_dtype=F32, tm=tm, tn=LANES, name="proj_ab")
    gates = _norm_matmul(x2, g1, w_gate.astype(BF16), _row(b_gate), act="sigmoid", out_dtype=BF16,
                         tm=tm, tn=1024, name="gates")

    gq = _row(jnp.tile(da_qnorm_g, 2))
    gk = _row(jnp.tile(da_knorm_g, 2))
    qz, kr = _qk_prep(proj, pos2, inv_row, gq, gk, tm=_pick(t, 512))
    lam_pack = jnp.zeros((8, LANES), F32)
    lam_pack = lam_pack.at[0:4, 0:DA_QK_DIM].set(jnp.stack([lq1, lk1, lq2, lk2]).astype(F32))
    lam_init = 0.8 - 0.6 * math.exp(-0.3 * l)
    oa = _diff_attn(qz, kr, proj, lam_pack, _row(da_subln_g), batch=batch, seq=seq,
                    tq=_pick(seq, 1024), tk=_pick(seq, 512), sub=128, ahead=3, lam_init=lam_init)

    alog_row = jnp.zeros((1, LANES), F32).at[0, AB_A0:AB_A0 + GDN_HEADS].set(gdn_a_log.astype(F32))
    dtb_row = jnp.zeros((1, LANES), F32).at[0, AB_A0:AB_A0 + GDN_HEADS].set(gdn_dt_bias.astype(F32))
    pq, cmat, cd, op = _gdn_prep(proj, ab, w_conv.astype(F32), alog_row, dtb_row,
                                 batch=batch, seq=seq, ts=_pick(seq, 256), cpb=2)
    ob = _gdn_scan(pq, cmat, cd, op, proj, _row(gdn_norm_g), batch=batch, seq=seq, ts=_pick(seq, 512))

    kc, vc = _mem_kv(mem2, _row(mem_norm_g), w_mem_kv.astype(BF16), _row(ca_knorm_g),
                     batch=batch, n_mem=n_mem)
    oc = _cross_attn(proj, kc, vc, _row(ca_qnorm_g), batch=batch, seq=seq, n_mem=n_mem,
                     tq=_pick(seq, 1024))

    x2 = _merge(x2, oa, ob, oc, gates, w_o_diff.astype(BF16), w_o_delta.astype(BF16),
                w_o_cross.astype(BF16), w_out.astype(BF16), tm=_pick(t, 512))
    x2 = _mlp(x2, _row(ln2_g), w_mlp1.astype(BF16), w_mlp2.astype(BF16), tm=_pick(t, 1024), tf=1024)
    return x2


def kernel(x, mem, positions, ln1_g, w_in, w_gate, b_gate, da_qnorm_g, da_knorm_g, da_lambda_q1, da_lambda_k1, da_lambda_q2, da_lambda_k2, da_subln_g, w_o_diff, w_conv, gdn_a_log, gdn_dt_bias, gdn_norm_g, w_o_delta, mem_norm_g, w_mem_kv, ca_qnorm_g, ca_knorm_g, w_o_cross, w_out, ln2_g, w_mlp1, w_mlp2):
    batch, seq, d = x.shape
    n_mem = mem.shape[1]
    x2 = x.reshape(batch * seq, d)
    mem2 = mem.reshape(batch * n_mem, d)
    pos2 = positions.reshape(batch * seq, 1)
    half = DA_QK_DIM // 2
    inv = jnp.exp(-math.log(ROPE_THETA) * jnp.arange(half, dtype=F32) / half)
    inv_row = jnp.tile(inv, LANES // half).reshape(1, LANES)
    params = (ln1_g, w_in, w_gate, b_gate, da_qnorm_g, da_knorm_g, da_lambda_q1, da_lambda_k1,
              da_lambda_q2, da_lambda_k2, da_subln_g, w_o_diff, w_conv, gdn_a_log, gdn_dt_bias,
              gdn_norm_g, w_o_delta, mem_norm_g, w_mem_kv, ca_qnorm_g, ca_knorm_g, w_o_cross, w_out,
              ln2_g, w_mlp1, w_mlp2)
    for l in range(ln1_g.shape[0]):
        x2 = _layer(x2, mem2, pos2, inv_row, batch, seq, n_mem, l, tuple(a[l] for a in params))
    return x2.reshape(batch, seq, d)
```

```python
import functools
import math

import jax
import jax.numpy as jnp
from jax import lax
from jax.experimental import pallas as pl
from jax.experimental.pallas import tpu as pltpu

F32 = jnp.float32
BF16 = jnp.bfloat16

EPS = 1e-6
CHUNK = 64
ROPE_THETA = 10000.0
N_BRANCH = 3

DA_HEADS = 6
DA_QK_DIM = 64
DA_V_DIM = 128
GDN_HEADS = 6
GDN_K_DIM = 128
GDN_V_DIM = 128
CONV_K = 4
CA_HEADS = 4
CA_DIM = 128

LANES = 128
HW = GDN_HEADS * GDN_K_DIM
CA_Q = CA_HEADS * CA_DIM

MXU_N = 256

COL_GQ, COL_GK, COL_GV, COL_GZ, COL_DV, COL_CQ = 0, HW, 2 * HW, 3 * HW, 4 * HW, 5 * HW
N_PROJ = 5 * HW + CA_Q
WCOL_DQ, WCOL_DK, WCOL_AB = N_PROJ, N_PROJ + HW, N_PROJ + 2 * HW
WCOL_GATE = WCOL_AB + MXU_N
AB_B0, AB_A0 = 0, 8

VMEM_LIMIT = 48 * 1024 * 1024
FRONT_VMEM_LIMIT = 56 * 1024 * 1024


def _cparams(sem):
    return pltpu.CompilerParams(dimension_semantics=sem, vmem_limit_bytes=VMEM_LIMIT)


def _nt_dot(a, b):
    return lax.dot_general(a, b, (((1,), (1,)), ((), ())), preferred_element_type=F32)


def _tn_dot(a, b):
    return lax.dot_general(a, b, (((0,), (0,)), ((), ())), preferred_element_type=F32)


def _dot(a, b):
    return jnp.dot(a, b, preferred_element_type=F32)


def _split_bf16(x):
    hi = x.astype(BF16)
    r = x - hi.astype(F32)
    mid = r.astype(BF16)
    lo = (r - mid.astype(F32)).astype(BF16)
    return hi, mid, lo


def _front_kernel(x_ref, g_ref, w_ref, bg_ref, pos_ref, inv_ref, gq_ref, gk_ref,
                  proj_ref, ab_ref, gates_ref, qz_ref, kr_ref):
    tm = x_ref.shape[0]
    x = x_ref[...]
    h = (x * lax.rsqrt(jnp.mean(x * x, axis=-1, keepdims=True) + EPS) * g_ref[...]).astype(BF16)

    def cols(c0, n=MXU_N):
        return _dot(h, w_ref[:, c0:c0 + n])

    for c0 in range(0, N_PROJ, MXU_N):
        proj_ref[:, c0:c0 + MXU_N] = cols(c0).astype(proj_ref.dtype)

    ang = pos_ref[...].astype(F32) * inv_ref[...]
    cos = jnp.cos(ang)
    sin = jnp.sin(ang)
    lane = lax.broadcasted_iota(jnp.int32, (tm, LANES), 1)
    lo_map = lane < DA_QK_DIM
    first_half = (lane % DA_QK_DIM) < (DA_QK_DIM // 2)
    sin_signed = jnp.where(first_half, -sin, sin)
    scale = DA_QK_DIM ** -0.5 * math.log2(math.e)

    def norm_rope(x, g):
        s = x * x
        tot = jnp.sum(s, axis=-1, keepdims=True)
        lo = jnp.sum(jnp.where(lo_map, s, 0.0), axis=-1, keepdims=True)
        ms = jnp.where(lo_map, lo, tot - lo) * (1.0 / DA_QK_DIM)
        y = x * lax.rsqrt(ms + EPS) * g
        partner = jnp.where(first_half,
                            pltpu.roll(y, LANES - DA_QK_DIM // 2, axis=1),
                            pltpu.roll(y, DA_QK_DIM // 2, axis=1))
        return y * cos + partner * sin_signed

    for c0 in range(0, HW, MXU_N):
        yq = cols(WCOL_DQ + c0)
        yk = cols(WCOL_DK + c0)
        for j in range(MXU_N // LANES):
            sl = slice(c0 + j * LANES, c0 + (j + 1) * LANES)
            q = norm_rope(yq[:, j * LANES:(j + 1) * LANES], gq_ref[...]) * scale
            qz_ref[0, :, sl] = jnp.where(lo_map, q, 0.0).astype(qz_ref.dtype)
            qz_ref[1, :, sl] = jnp.where(lo_map, 0.0, q).astype(qz_ref.dtype)
            k = norm_rope(yk[:, j * LANES:(j + 1) * LANES], gk_ref[...])
            kr_ref[:, sl] = k.astype(kr_ref.dtype)

    ab_ref[...] = cols(WCOL_AB)[:, :LANES]

    for c0 in range(0, gates_ref.shape[1], MXU_N):
        y = cols(WCOL_GATE + c0) + bg_ref[:, c0:c0 + MXU_N]
        gates_ref[:, c0:c0 + MXU_N] = jax.nn.sigmoid(y).astype(gates_ref.dtype)


def _front(x, g, w_all, b_gate, pos, inv, gq, gk, *, tm):
    t, d = x.shape
    n_gate = b_gate.shape[1]

    def tok(n):
        return pl.BlockSpec((tm, n), lambda i: (i, 0))

    def const(shape, **kw):
        return pl.BlockSpec(shape, lambda i: (0,) * len(shape), **kw)

    return pl.pallas_call(
        _front_kernel,
        grid=(t // tm,),
        in_specs=[
            tok(d), const((1, d)),
            const(w_all.shape, pipeline_mode=pl.Buffered(1)),
            const((1, n_gate)), tok(1), const((1, LANES)), const((1, LANES)), const((1, LANES)),
        ],
        out_specs=[
            tok(N_PROJ), tok(LANES), tok(n_gate),
            pl.BlockSpec((2, tm, HW), lambda i: (0, i, 0)),
            tok(HW),
        ],
        out_shape=[
            jax.ShapeDtypeStruct((t, N_PROJ), BF16),
            jax.ShapeDtypeStruct((t, LANES), F32),
            jax.ShapeDtypeStruct((t, n_gate), BF16),
            jax.ShapeDtypeStruct((2, t, HW), BF16),
            jax.ShapeDtypeStruct((t, HW), BF16),
        ],
        compiler_params=pltpu.CompilerParams(dimension_semantics=("parallel",),
                                             vmem_limit_bytes=FRONT_VMEM_LIMIT),
        name="front",
    )(x, g, w_all, b_gate, pos, inv, gq, gk)


def _diff_attn_kernel(qz_ref, k_ref, v_ref, lam_ref, g_ref, o_ref, m_ref, l_ref, acc_ref,
                      *, tq, tk, seg, sub, ahead, lam_init):
    i = pl.program_id(2)
    n_sub = 2 * tq // sub
    m_ref[...] = jnp.full(m_ref.shape, -jnp.inf, F32)
    l_ref[...] = jnp.zeros(l_ref.shape, F32)
    acc_ref[...] = jnp.zeros(acc_ref.shape, F32)

    def scores(base, item):
        r, k0, nk, _ = item
        mp, off = divmod(r * sub, tq)
        return _nt_dot(qz_ref[mp, off:off + sub, :], k_ref[pl.ds(base + k0, nk), :])

    def consume(base, item, s):
        r, k0, nk, masked = item
        off = (r * sub) % tq
        rs = slice(r * sub, (r + 1) * sub)
        if masked:
            row = lax.broadcasted_iota(jnp.int32, s.shape, 0) + off
            col = lax.broadcasted_iota(jnp.int32, s.shape, 1) + k0
            s = jnp.where((col // CHUNK) <= (row // CHUNK), s, -jnp.inf)
        m_prev = m_ref[rs]
        m_new = jnp.maximum(m_prev, jnp.max(s, axis=-1, keepdims=True))
        alpha = jnp.exp2(m_prev - m_new)
        p = jnp.exp2(s - jnp.concatenate([m_new] * (nk // LANES), axis=1))
        psum = p[:, 0:LANES]
        for c in range(1, nk // LANES):
            psum = psum + p[:, c * LANES:(c + 1) * LANES]
        l_ref[rs] = alpha * l_ref[rs] + psum
        v = v_ref[pl.ds(base + k0, nk), :]
        acc_ref[rs] = alpha * acc_ref[rs] + _dot(p.astype(v.dtype), v)
        m_ref[rs] = m_new

    def pipelined(base, items):
        pending = [scores(base, it) for it in items[:ahead]]
        for n, it in enumerate(items):
            s = pending.pop(0)
            if n + ahead < len(items):
                pending.append(scores(base, items[n + ahead]))
            consume(base, it, s)

    full_items = [(r, kb * seg, seg, False) for kb in range(tk // seg) for r in range(n_sub)]

    def body(j, carry):
        pipelined(pl.multiple_of(j * tk, tk), full_items)
        return carry

    lax.fori_loop(0, i * (tq // tk), body, 0)

    diag_items = []
    for kb in range(tq // seg):
        for r in range(n_sub):
            off = (r * sub) % tq
            k0, k1 = kb * seg, min((kb + 1) * seg, off + sub)
            if k1 > k0:
                diag_items.append((r, k0, k1 - k0, k1 > off))
    pipelined(pl.multiple_of(i * tq, tq), diag_items)

    o = acc_ref[...] / jnp.sum(l_ref[...], axis=-1, keepdims=True)
    lp = lam_ref[...]
    lam = (jnp.exp(jnp.sum(lp[0:1] * lp[1:2], axis=-1, keepdims=True))
           - jnp.exp(jnp.sum(lp[2:3] * lp[3:4], axis=-1, keepdims=True)) + lam_init)
    d = o[:tq] - lam * o[tq:]
    ms = jnp.mean(d * d, axis=-1, keepdims=True)
    o_ref[...] = (d * lax.rsqrt(ms + EPS) * g_ref[...] * (1.0 - lam_init)).astype(o_ref.dtype)


def _diff_attn(qz, kr, proj, lam_pack, g, *, batch, seq, tq, tk, seg, sub, ahead, lam_init):
    t = batch * seq
    nq = seq // tq
    return pl.pallas_call(
        functools.partial(_diff_attn_kernel, tq=tq, tk=tk, seg=seg, sub=sub, ahead=ahead,
                          lam_init=lam_init),
        grid=(batch, DA_HEADS, nq),
        in_specs=[
            pl.BlockSpec((2, tq, LANES), lambda b, h, i: (0, b * nq + i, h)),
            pl.BlockSpec((seq, LANES), lambda b, h, i: (b, h)),
            pl.BlockSpec((seq, LANES), lambda b, h, i: (b, COL_DV // LANES + h)),
            pl.BlockSpec((8, LANES), lambda b, h, i: (0, 0)),
            pl.BlockSpec((1, LANES), lambda b, h, i: (0, 0)),
        ],
        out_specs=pl.BlockSpec((tq, LANES), lambda b, h, i: (b * nq + i, h)),
        out_shape=jax.ShapeDtypeStruct((t, HW), BF16),
        scratch_shapes=[
            pltpu.VMEM((2 * tq, LANES), F32),
            pltpu.VMEM((2 * tq, LANES), F32),
            pltpu.VMEM((2 * tq, LANES), F32),
        ],
        compiler_params=_cparams(("parallel", "parallel", "arbitrary")),
        name="diff_attn",
    )(qz, kr, proj, lam_pack, g)


def _gdn_prep_kernel(q_ref, k_ref, v_ref, qh_ref, kh_ref, vh_ref, ab_ref, wc_ref, alog_ref, dtb_ref,
                     pq_ref, c_ref, cd_ref, op_ref,
                     xp_ref, qn_ref, kn_ref, vn_ref, beta_ref, g_ref, grow_ref, *, ts, cpb):
    first = pl.program_id(1) == 0
    nc = ts // CHUNK

    for s, (cur_ref, halo_ref, dst_ref) in enumerate(
            ((q_ref, qh_ref, qn_ref), (k_ref, kh_ref, kn_ref), (v_ref, vh_ref, vn_ref))):
        halo = halo_ref[...].astype(F32)
        xp_ref[0:8, :] = jnp.where(first, 0.0, halo)
        xp_ref[8:, :] = cur_ref[...].astype(F32)
        acc = xp_ref[8:8 + ts, :] * wc_ref[CONV_K - 1:CONV_K, s * HW:(s + 1) * HW]
        for j in range(1, CONV_K):
            acc = acc + xp_ref[8 - j:8 - j + ts, :] * wc_ref[CONV_K - 1 - j:CONV_K - j, s * HW:(s + 1) * HW]
        act = acc * jax.nn.sigmoid(acc)
        if s < 2:
            for h in range(GDN_HEADS):
                sl = slice(h * LANES, (h + 1) * LANES)
                a_h = act[:, sl]
                dst_ref[:, sl] = a_h * lax.rsqrt(jnp.sum(a_h * a_h, axis=-1, keepdims=True) + EPS)
        else:
            dst_ref[...] = act

    ab = ab_ref[...]
    beta_ref[...] = jax.nn.sigmoid(ab)
    xa = ab + dtb_ref[...]
    softplus = jnp.maximum(xa, 0.0) + jnp.log1p(jnp.exp(-jnp.abs(xa)))
    la = -jnp.exp(alog_ref[...]) * softplus
    ti = lax.broadcasted_iota(jnp.int32, (ts, ts), 0)
    tj = lax.broadcasted_iota(jnp.int32, (ts, ts), 1)
    ltri = jnp.where((tj <= ti) & ((ti // CHUNK) == (tj // CHUNK)), 1.0, 0.0).astype(BF16)
    g_all = sum(_dot(ltri, part) for part in _split_bf16(la))
    g_ref[...] = g_all
    sel_r = lax.broadcasted_iota(jnp.int32, (8, LANES), 0)
    sel_c = lax.broadcasted_iota(jnp.int32, (8, LANES), 1)
    sel = jnp.where(sel_c == sel_r + AB_A0, 1.0, 0.0).astype(BF16)
    grow_all = sum(_nt_dot(sel, part) for part in _split_bf16(g_all))
    for c in range(nc):
        grow_ref[c] = grow_all[:, c * CHUNK:(c + 1) * CHUNK]

    ri = lax.broadcasted_iota(jnp.int32, (CHUNK, CHUNK), 0)
    ci = lax.broadcasted_iota(jnp.int32, (CHUNK, CHUNK), 1)
    incl = ci <= ri
    strict = ci < ri
    scale = GDN_K_DIM ** -0.5

    eye = jnp.where(ri == ci, 1.0, 0.0)
    same_blk = (ri // 16) == (ci // 16)

    def bdot(a, b):
        return _dot(a.astype(BF16), b.astype(BF16))

    def group_body(cg, carry):
        us = []
        for cc in range(cpb):
            c = cg * cpb + cc
            r0 = pl.multiple_of(c * CHUNK, CHUNK)
            rows = pl.ds(r0, CHUNK)
            for h in range(GDN_HEADS):
                sl = slice(h * LANES, (h + 1) * LANES)
                gcol = g_ref[rows, AB_A0 + h:AB_A0 + h + 1]
                glast = g_ref[pl.ds(r0 + CHUNK - 1, 1), AB_A0 + h:AB_A0 + h + 1]
                grow = grow_ref[c, h:h + 1, :]
                us.append(dict(
                    c=c, h=h, rows=rows, sl=sl, gcol=gcol, glast=glast,
                    q=qn_ref[rows, sl] * scale, k=kn_ref[rows, sl], v=vn_ref[rows, sl],
                    b=beta_ref[rows, AB_B0 + h:AB_B0 + h + 1], eg=jnp.exp(gcol),
                    decay=jnp.where(incl, jnp.exp(jnp.where(incl, gcol - grow, 0.0)), 0.0)))

        for u in us:
            kb = u["k"].astype(BF16)
            u["qk_kk"] = _nt_dot(jnp.concatenate([u["q"].astype(BF16), kb], axis=0), kb)
        for u in us:
            a = jnp.where(strict, u["b"] * u["qk_kk"][CHUNK:] * u["decay"], 0.0)
            ad = jnp.where(same_blk, a, 0.0)
            u["ao"] = a - ad
            u["x"] = eye - ad
            u["ad"] = ad
        for u in us:
            u["p"] = bdot(u["ad"], u["ad"])
        for _ in range(2):
            for u in us:
                xp = bdot(u["x"], u["p"])
                u["p"] = bdot(u["p"], u["p"])
                u["x"] = u["x"] + xp
        for u in us:
            u["x"] = u["x"] + bdot(u["x"], u["p"])
        for u in us:
            rhs = jnp.concatenate([u["k"] * (u["b"] * u["eg"]), u["v"] * u["b"]], axis=1)
            u["bm"] = bdot(u["x"], u["ao"])
            u["z"] = bdot(u["x"], rhs)
        for u in us:
            u["b2"] = bdot(u["bm"], u["bm"])
        for u in us:
            u["z"] = u["z"] + bdot(u["b2"], u["z"])
        for u in us:
            u["wu"] = (u["z"] - bdot(u["bm"], u["z"])).astype(BF16)
        for u in us:
            qkm = (u["qk_kk"][:CHUNK] * u["decay"]).astype(BF16)
            kd = (u["k"] * jnp.exp(u["glast"] - u["gcol"])).astype(BF16)
            u["r2"] = _dot(qkm, u["wu"])
            u["r3"] = _tn_dot(kd, u["wu"])
        for u in us:
            c, h = u["c"], u["h"]
            pq_ref[c, h, 0:LANES, :] = u["r3"][:, :LANES].astype(pq_ref.dtype)
            pq_ref[c, h, LANES:, :] = (u["q"] * u["eg"] - u["r2"][:, :LANES]).astype(pq_ref.dtype)
            c_ref[c, h] = u["r3"][:, LANES:].astype(c_ref.dtype)
            cd_ref[c, h] = jnp.broadcast_to(jnp.exp(u["glast"]), (8, LANES))
            op_ref[u["rows"], u["sl"]] = u["r2"][:, LANES:].astype(op_ref.dtype)
        return carry

    lax.fori_loop(0, nc // cpb, group_body, 0)


def _gdn_prep(proj, ab, wconv, alog_row, dtb_row, *, batch, seq, ts, cpb):
    t = batch * seq
    nt = seq // ts
    nc = ts // CHUNK
    n_chunks = t // CHUNK
    hb = ts // 8

    def cur(col):
        return pl.BlockSpec((ts, HW), lambda b, i: (b * nt + i, col // HW))

    def halo(col):
        return pl.BlockSpec((8, HW), lambda b, i: (jnp.maximum((b * nt + i) * hb - 1, 0), col // HW))

    return pl.pallas_call(
        functools.partial(_gdn_prep_kernel, ts=ts, cpb=cpb),
        grid=(batch, nt),
        in_specs=[
            cur(COL_GQ), cur(COL_GK), cur(COL_GV), halo(COL_GQ), halo(COL_GK), halo(COL_GV),
            pl.BlockSpec((ts, LANES), lambda b, i: (b * nt + i, 0)),
            pl.BlockSpec((CONV_K, 3 * HW), lambda b, i: (0, 0)),
            pl.BlockSpec((1, LANES), lambda b, i: (0, 0)),
            pl.BlockSpec((1, LANES), lambda b, i: (0, 0)),
        ],
        out_specs=[
            pl.BlockSpec((nc, GDN_HEADS, GDN_K_DIM + CHUNK, LANES), lambda b, i: (b * nt + i, 0, 0, 0)),
            pl.BlockSpec((nc, GDN_HEADS, GDN_K_DIM, LANES), lambda b, i: (b * nt + i, 0, 0, 0)),
            pl.BlockSpec((nc, GDN_HEADS, 8, LANES), lambda b, i: (b * nt + i, 0, 0, 0)),
            pl.BlockSpec((ts, HW), lambda b, i: (b * nt + i, 0)),
        ],
        out_shape=[
            jax.ShapeDtypeStruct((n_chunks, GDN_HEADS, GDN_K_DIM + CHUNK, LANES), BF16),
            jax.ShapeDtypeStruct((n_chunks, GDN_HEADS, GDN_K_DIM, LANES), BF16),
            jax.ShapeDtypeStruct((n_chunks, GDN_HEADS, 8, LANES), F32),
            jax.ShapeDtypeStruct((t, HW), BF16),
        ],
        scratch_shapes=[
            pltpu.VMEM((ts + 8, HW), F32),
            pltpu.VMEM((ts, HW), F32),
            pltpu.VMEM((ts, HW), F32),
            pltpu.VMEM((ts, HW), F32),
            pltpu.VMEM((ts, LANES), F32),
            pltpu.VMEM((ts, LANES), F32),
            pltpu.VMEM((nc, 8, CHUNK), F32),
        ],
        compiler_params=_cparams(("parallel", "parallel")),
        name="gdn_prep",
    )(proj, proj, proj, proj, proj, proj, ab, wconv, alog_row, dtb_row)


def _gdn_scan_kernel(pq_ref, c_ref, cd_ref, op_ref, z_ref, gn_ref, o_ref, s_ref, *, ts):
    @pl.when(pl.program_id(1) == 0)
    def _():
        s_ref[...] = jnp.zeros(s_ref.shape, F32)

    nc = ts // CHUNK

    def chunk_body(c, carry):
        r0 = pl.multiple_of(c * CHUNK, CHUNK)
        rows = pl.ds(r0, CHUNK)
        for h in range(GDN_HEADS):
            sl = slice(h * LANES, (h + 1) * LANES)
            s = s_ref[h]
            r = _dot(pq_ref[c, h], s.astype(BF16))
            o = r[GDN_K_DIM:] + op_ref[rows, sl].astype(F32)
            s_ref[h] = cd_ref[c, h, 0:1, :] * s - r[:GDN_K_DIM] + c_ref[c, h].astype(F32)
            on = o * lax.rsqrt(jnp.mean(o * o, axis=-1, keepdims=True) + EPS) * gn_ref[...]
            z = z_ref[rows, sl].astype(F32)
            o_ref[rows, sl] = (on * (z * jax.nn.sigmoid(z))).astype(o_ref.dtype)
        return carry

    lax.fori_loop(0, nc, chunk_body, 0)


def _gdn_scan(pq, cmat, cd, op, proj, gn, *, batch, seq, ts):
    t = batch * seq
    nt = seq // ts
    nc = ts // CHUNK
    return pl.pallas_call(
        functools.partial(_gdn_scan_kernel, ts=ts),
        grid=(batch, nt),
        in_specs=[
            pl.BlockSpec((nc, GDN_HEADS, GDN_K_DIM + CHUNK, LANES), lambda b, i: (b * nt + i, 0, 0, 0)),
            pl.BlockSpec((nc, GDN_HEADS, GDN_K_DIM, LANES), lambda b, i: (b * nt + i, 0, 0, 0)),
            pl.BlockSpec((nc, GDN_HEADS, 8, LANES), lambda b, i: (b * nt + i, 0, 0, 0)),
            pl.BlockSpec((ts, HW), lambda b, i: (b * nt + i, 0)),
            pl.BlockSpec((ts, HW), lambda b, i: (b * nt + i, COL_GZ // HW)),
            pl.BlockSpec((1, LANES), lambda b, i: (0, 0)),
        ],
        out_specs=pl.BlockSpec((ts, HW), lambda b, i: (b * nt + i, 0)),
        out_shape=jax.ShapeDtypeStruct((t, HW), BF16),
        scratch_shapes=[pltpu.VMEM((GDN_HEADS, GDN_K_DIM, GDN_V_DIM), F32)],
        compiler_params=_cparams(("parallel", "arbitrary")),
        name="gdn_scan",
    )(pq, cmat, cd, op, proj, gn)


def _mem_kv_kernel(m_ref, g_ref, w_ref, gk_ref, k_ref, v_ref):
    m = m_ref[...]
    mn = m * lax.rsqrt(jnp.mean(m * m, axis=-1, keepdims=True) + EPS) * g_ref[...]
    kv = _dot(mn.astype(BF16), w_ref[...])
    for h in range(CA_HEADS):
        sl = slice(h * LANES, (h + 1) * LANES)
        kh = kv[:, sl]
        k_ref[:, sl] = (kh * lax.rsqrt(jnp.mean(kh * kh, axis=-1, keepdims=True) + EPS)
                        * gk_ref[...]).astype(k_ref.dtype)
    v_ref[...] = kv[:, CA_Q:].astype(v_ref.dtype)


def _mem_kv(mem2, g, w, gk, *, batch, n_mem):
    d = mem2.shape[1]
    return pl.pallas_call(
        _mem_kv_kernel,
        grid=(batch,),
        in_specs=[
            pl.BlockSpec((n_mem, d), lambda b: (b, 0)),
            pl.BlockSpec((1, d), lambda b: (0, 0)),
            pl.BlockSpec((d, 2 * CA_Q), lambda b: (0, 0)),
            pl.BlockSpec((1, LANES), lambda b: (0, 0)),
        ],
        out_specs=[
            pl.BlockSpec((n_mem, CA_Q), lambda b: (b, 0)),
            pl.BlockSpec((n_mem, CA_Q), lambda b: (b, 0)),
        ],
        out_shape=[
            jax.ShapeDtypeStruct((batch * n_mem, CA_Q), BF16),
            jax.ShapeDtypeStruct((batch * n_mem, CA_Q), BF16),
        ],
        compiler_params=_cparams(("parallel",)),
        name="mem_kv",
    )(mem2, g, w, gk)


def _cross_attn_kernel(q_ref, k_ref, v_ref, g_ref, o_ref):
    q = q_ref[...].astype(F32)
    qn = q * lax.rsqrt(jnp.mean(q * q, axis=-1, keepdims=True) + EPS) * g_ref[...] * (CA_DIM ** -0.5)
    s = _nt_dot(qn.astype(BF16), k_ref[...])
    p = jnp.exp(s - jnp.max(s, axis=-1, keepdims=True))
    l = jnp.sum(p, axis=-1, keepdims=True)
    o_ref[...] = (_dot(p.astype(BF16), v_ref[...]) / l).astype(o_ref.dtype)


def _cross_attn(proj, kc, vc, g, *, batch, seq, n_mem, tq):
    t = batch * seq
    nq = seq // tq
    return pl.pallas_call(
        _cross_attn_kernel,
        grid=(batch, nq, CA_HEADS),
        in_specs=[
            pl.BlockSpec((tq, LANES), lambda b, i, h: (b * nq + i, COL_CQ // LANES + h)),
            pl.BlockSpec((n_mem, LANES), lambda b, i, h: (b, h)),
            pl.BlockSpec((n_mem, LANES), lambda b, i, h: (b, h)),
            pl.BlockSpec((1, LANES), lambda b, i, h: (0, 0)),
        ],
        out_specs=pl.BlockSpec((tq, LANES), lambda b, i, h: (b * nq + i, h)),
        out_shape=jax.ShapeDtypeStruct((t, CA_Q), BF16),
        compiler_params=_cparams(("parallel", "parallel", "parallel")),
        name="cross_attn",
    )(proj, kc, vc, g)


def _merge_kernel(x_ref, oa_ref, ob_ref, oc_ref, gt_ref, wa_ref, wb_ref, wc_ref, wo_ref, o_ref):
    d = x_ref.shape[1]
    ya = _dot(oa_ref[...], wa_ref[...])
    yb = _dot(ob_ref[...], wb_ref[...])
    yc = _dot(oc_ref[...], wc_ref[...])
    mixed = (gt_ref[:, 0:d].astype(F32) * ya + gt_ref[:, d:2 * d].astype(F32) * yb
             + gt_ref[:, 2 * d:3 * d].astype(F32) * yc)
    o_ref[...] = x_ref[...] + _dot(mixed.astype(BF16), wo_ref[...])


def _merge(x, oa, ob, oc, gates, wa, wb, wc, wo, *, tm):
    t, d = x.shape

    def tok(n):
        return pl.BlockSpec((tm, n), lambda i: (i, 0))

    def full(a):
        return pl.BlockSpec(a.shape, lambda i: (0, 0))

    return pl.pallas_call(
        _merge_kernel,
        grid=(t // tm,),
        in_specs=[tok(d), tok(HW), tok(HW), tok(CA_Q), tok(N_BRANCH * d),
                  full(wa), full(wb), full(wc), full(wo)],
        out_specs=tok(d),
        out_shape=jax.ShapeDtypeStruct((t, d), F32),
        compiler_params=_cparams(("parallel",)),
        name="merge",
    )(x, oa, ob, oc, gates, wa, wb, wc, wo)


def _mlp_kernel(x_ref, g_ref, w1_ref, w2_ref, o_ref, h_ref, acc_ref):
    k = pl.program_id(1)

    @pl.when(k == 0)
    def _():
        x = x_ref[...]
        ms = jnp.mean(x * x, axis=-1, keepdims=True)
        h_ref[...] = (x * lax.rsqrt(ms + EPS) * g_ref[...]).astype(h_ref.dtype)
        acc_ref[...] = x

    a = jnp.maximum(_dot(h_ref[...], w1_ref[...]), 0.0)
    acc_ref[...] += _dot((a * a).astype(BF16), w2_ref[...])

    @pl.when(k == pl.num_programs(1) - 1)
    def _():
        o_ref[...] = acc_ref[...]


def _mlp(x, g, w1, w2, *, tm, tf):
    t, d = x.shape
    f = w1.shape[1]
    return pl.pallas_call(
        _mlp_kernel,
        grid=(t // tm, f // tf),
        in_specs=[
            pl.BlockSpec((tm, d), lambda i, k: (i, 0)),
            pl.BlockSpec((1, d), lambda i, k: (0, 0)),
            pl.BlockSpec((d, tf), lambda i, k: (0, k)),
            pl.BlockSpec((tf, d), lambda i, k: (k, 0)),
        ],
        out_specs=pl.BlockSpec((tm, d), lambda i, k: (i, 0)),
        out_shape=jax.ShapeDtypeStruct((t, d), F32),
        scratch_shapes=[pltpu.VMEM((tm, d), BF16), pltpu.VMEM((tm, d), F32)],
        compiler_params=_cparams(("parallel", "arbitrary")),
        name="mlp",
    )(x, g, w1, w2)


def _pick(n, pref):
    t = min(pref, n)
    while n % t:
        t //= 2
    return t


def _row(v, width=None):
    v = v.astype(F32).reshape(1, -1)
    if width is not None and v.shape[1] < width:
        v = jnp.pad(v, ((0, 0), (0, width - v.shape[1])))
    return v


def _layer(x2, mem2, pos2, inv_row, batch, seq, n_mem, l, p):
    t, d = x2.shape
    (ln1_g, w_in, w_gate, b_gate, da_qnorm_g, da_knorm_g, lq1, lk1, lq2, lk2, da_subln_g, w_o_diff,
     w_conv, gdn_a_log, gdn_dt_bias, gdn_norm_g, w_o_delta, mem_norm_g, w_mem_kv, ca_qnorm_g,
     ca_knorm_g, w_o_cross, w_out, ln2_g, w_mlp1, w_mlp2) = p

    o = [0]
    for n in (HW, HW, HW, HW, HW, HW, HW, GDN_HEADS, GDN_HEADS, CA_Q):
        o.append(o[-1] + n)
    sec = [w_in[:, o[i]:o[i + 1]] for i in range(10)]
    w_ab = jnp.zeros((d, MXU_N), F32)
    w_ab = w_ab.at[:, AB_B0:AB_B0 + GDN_HEADS].set(sec[7]).at[:, AB_A0:AB_A0 + GDN_HEADS].set(sec[8])
    w_all = jnp.concatenate([sec[3], sec[4], sec[5], sec[6], sec[2], sec[9], sec[0], sec[1], w_ab, w_gate],
                            axis=1).astype(BF16)
    gq = _row(jnp.tile(da_qnorm_g, 2))
    gk = _row(jnp.tile(da_knorm_g, 2))
    proj, ab, gates, qz, kr = _front(x2, _row(ln1_g), w_all, _row(b_gate), pos2, inv_row, gq, gk,
                                     tm=_pick(t, 512))

    lam_pack = jnp.zeros((8, LANES), F32)
    lam_pack = lam_pack.at[0:4, 0:DA_QK_DIM].set(jnp.stack([lq1, lk1, lq2, lk2]).astype(F32))
    lam_init = 0.8 - 0.6 * math.exp(-0.3 * l)
    oa = _diff_attn(qz, kr, proj, lam_pack, _row(da_subln_g), batch=batch, seq=seq,
                    tq=_pick(seq, 1024), tk=_pick(seq, 1024), seg=_pick(seq, 512), sub=256, ahead=2,
                    lam_init=lam_init)

    alog_row = jnp.zeros((1, LANES), F32).at[0, AB_A0:AB_A0 + GDN_HEADS].set(gdn_a_log.astype(F32))
    dtb_row = jnp.zeros((1, LANES), F32).at[0, AB_A0:AB_A0 + GDN_HEADS].set(gdn_dt_bias.astype(F32))
    pq, cmat, cd, op = _gdn_prep(proj, ab, w_conv.astype(F32), alog_row, dtb_row,
                                 batch=batch, seq=seq, ts=_pick(seq, 256), cpb=2)
    ob = _gdn_scan(pq, cmat, cd, op, proj, _row(gdn_norm_g), batch=batch, seq=seq, ts=_pick(seq, 512))

    kc, vc = _mem_kv(mem2, _row(mem_norm_g), w_mem_kv.astype(BF16), _row(ca_knorm_g),
                     batch=batch, n_mem=n_mem)
    oc = _cross_attn(proj, kc, vc, _row(ca_qnorm_g), batch=batch, seq=seq, n_mem=n_mem,
                     tq=_pick(seq, 1024))

    x2 = _merge(x2, oa, ob, oc, gates, w_o_diff.astype(BF16), w_o_delta.astype(BF16),
                w_o_cross.astype(BF16), w_out.astype(BF16), tm=_pick(t, 512))
    x2 = _mlp(x2, _row(ln2_g), w_mlp1.astype(BF16), w_mlp2.astype(BF16), tm=_pick(t, 1024), tf=1024)
    return x2


def kernel(x, mem, positions, ln1_g, w_in, w_gate, b_gate, da_qnorm_g, da_knorm_g, da_lambda_q1, da_lambda_k1, da_lambda_q2, da_lambda_k2, da_subln_g, w_o_diff, w_conv, gdn_a_log, gdn_dt_bias, gdn_norm_g, w_o_delta, mem_norm_g, w_mem_kv, ca_qnorm_g, ca_knorm_g, w_o_cross, w_out, ln2_g, w_mlp1, w_mlp2):
    batch, seq, d = x.shape
    n_mem = mem.shape[1]
    x2 = x.reshape(batch * seq, d)
    mem2 = mem.reshape(batch * n_mem, d)
    pos2 = positions.reshape(batch * seq, 1)
    half = DA_QK_DIM // 2
    inv = jnp.exp(-math.log(ROPE_THETA) * jnp.arange(half, dtype=F32) / half)
    inv_row = jnp.tile(inv, LANES // half).reshape(1, LANES)
    params = (ln1_g, w_in, w_gate, b_gate, da_qnorm_g, da_knorm_g, da_lambda_q1, da_lambda_k1,
              da_lambda_q2, da_lambda_k2, da_subln_g, w_o_diff, w_conv, gdn_a_log, gdn_dt_bias,
              gdn_norm_g, w_o_delta, mem_norm_g, w_mem_kv, ca_qnorm_g, ca_knorm_g, w_o_cross, w_out,
              ln2_g, w_mlp1, w_mlp2)
    for l in range(ln1_g.shape[0]):
        x2 = _layer(x2, mem2, pos2, inv_row, batch, seq, n_mem, l, tuple(a[l] for a in params))
    return x2.reshape(batch, seq, d)
```

```python
import functools
import math

import jax
import jax.numpy as jnp
from jax import lax
from jax.experimental import pallas as pl
from jax.experimental.pallas import tpu as pltpu

F32 = jnp.float32
BF16 = jnp.bfloat16

EPS = 1e-6
CHUNK = 64
ROPE_THETA = 10000.0
N_BRANCH = 3

DA_HEADS = 6
DA_QK_DIM = 64
DA_V_DIM = 128
GDN_HEADS = 6
GDN_K_DIM = 128
GDN_V_DIM = 128
CONV_K = 4
CA_HEADS = 4
CA_DIM = 128

LANES = 128
HW = GDN_HEADS * GDN_K_DIM
CA_Q = CA_HEADS * CA_DIM

MXU_N = 256

COL_GQ, COL_GK, COL_GV, COL_GZ, COL_CQ, COL_DV = 0, HW, 2 * HW, 3 * HW, 4 * HW, 4 * HW + CA_Q
N_PROJ = 5 * HW + CA_Q
WCOL_DQ, WCOL_DK, WCOL_AB = N_PROJ, N_PROJ + HW, N_PROJ + 2 * HW
WCOL_GATE = WCOL_AB + MXU_N
AB_B0, AB_A0 = 0, 8

VMEM_LIMIT = 48 * 1024 * 1024
FRONT_VMEM_LIMIT = 56 * 1024 * 1024
BACK_VMEM_LIMIT = 58 * 1024 * 1024


def _cparams(sem):
    return pltpu.CompilerParams(dimension_semantics=sem, vmem_limit_bytes=VMEM_LIMIT)


def _nt_dot(a, b):
    return lax.dot_general(a, b, (((1,), (1,)), ((), ())), preferred_element_type=F32)


def _tn_dot(a, b):
    return lax.dot_general(a, b, (((0,), (0,)), ((), ())), preferred_element_type=F32)


def _dot(a, b):
    return jnp.dot(a, b, preferred_element_type=F32)


def _split_bf16(x):
    hi = x.astype(BF16)
    r = x - hi.astype(F32)
    mid = r.astype(BF16)
    lo = (r - mid.astype(F32)).astype(BF16)
    return hi, mid, lo


def _front_kernel(x_ref, g_ref, w_ref, bg_ref, pos_ref, inv_ref, gq_ref, gk_ref,
                  proj_ref, ab_ref, gates_ref, qz_ref, kr_ref):
    tm = x_ref.shape[0]
    x = x_ref[...]
    h = (x * lax.rsqrt(jnp.mean(x * x, axis=-1, keepdims=True) + EPS) * g_ref[...]).astype(BF16)

    def cols(c0, n=MXU_N):
        return _dot(h, w_ref[:, c0:c0 + n])

    ang = pos_ref[...].astype(F32) * inv_ref[...]
    cos = jnp.cos(ang)
    sin = jnp.sin(ang)
    lane = lax.broadcasted_iota(jnp.int32, (tm, LANES), 1)
    lo_map = lane < DA_QK_DIM
    first_half = (lane % DA_QK_DIM) < (DA_QK_DIM // 2)
    sin_signed = jnp.where(first_half, -sin, sin)
    scale = DA_QK_DIM ** -0.5 * math.log2(math.e)

    def norm_rope(x, g):
        s = x * x
        tot = jnp.sum(s, axis=-1, keepdims=True)
        lo = jnp.sum(jnp.where(lo_map, s, 0.0), axis=-1, keepdims=True)
        ms = jnp.where(lo_map, lo, tot - lo) * (1.0 / DA_QK_DIM)
        y = x * lax.rsqrt(ms + EPS) * g
        partner = jnp.where(first_half,
                            pltpu.roll(y, LANES - DA_QK_DIM // 2, axis=1),
                            pltpu.roll(y, DA_QK_DIM // 2, axis=1))
        return y * cos + partner * sin_signed

    for c0 in range(0, HW, MXU_N):
        yq = cols(WCOL_DQ + c0)
        yk = cols(WCOL_DK + c0)
        for j in range(MXU_N // LANES):
            sl = slice(c0 + j * LANES, c0 + (j + 1) * LANES)
            q = norm_rope(yq[:, j * LANES:(j + 1) * LANES], gq_ref[...]) * scale
            qz_ref[0, :, sl] = jnp.where(lo_map, q, 0.0).astype(qz_ref.dtype)
            qz_ref[1, :, sl] = jnp.where(lo_map, 0.0, q).astype(qz_ref.dtype)
            k = norm_rope(yk[:, j * LANES:(j + 1) * LANES], gk_ref[...])
            kr_ref[:, sl] = k.astype(kr_ref.dtype)

    ab_ref[...] = cols(WCOL_AB)[:, :LANES]

    for c0 in range(0, gates_ref.shape[1], MXU_N):
        y = cols(WCOL_GATE + c0) + bg_ref[:, c0:c0 + MXU_N]
        gates_ref[:, c0:c0 + MXU_N] = jax.nn.sigmoid(y).astype(gates_ref.dtype)

    for c0 in range(0, N_PROJ, MXU_N):
        proj_ref[:, c0:c0 + MXU_N] = cols(c0).astype(proj_ref.dtype)


def _front(x, g, w_all, b_gate, pos, inv, gq, gk, *, tm):
    t, d = x.shape
    n_gate = b_gate.shape[1]

    def tok(n):
        return pl.BlockSpec((tm, n), lambda i: (i, 0))

    def const(shape, **kw):
        return pl.BlockSpec(shape, lambda i: (0,) * len(shape), **kw)

    return pl.pallas_call(
        _front_kernel,
        grid=(t // tm,),
        in_specs=[
            tok(d), const((1, d)),
            const(w_all.shape, pipeline_mode=pl.Buffered(1)),
            const((1, n_gate)), tok(LANES), const((1, LANES)), const((1, LANES)), const((1, LANES)),
        ],
        out_specs=[
            tok(N_PROJ), tok(LANES), tok(n_gate),
            pl.BlockSpec((2, tm, HW), lambda i: (0, i, 0)),
            tok(HW),
        ],
        out_shape=[
            jax.ShapeDtypeStruct((t, N_PROJ), BF16),
            jax.ShapeDtypeStruct((t, LANES), F32),
            jax.ShapeDtypeStruct((t, n_gate), BF16),
            jax.ShapeDtypeStruct((2, t, HW), BF16),
            jax.ShapeDtypeStruct((t, HW), BF16),
        ],
        compiler_params=pltpu.CompilerParams(dimension_semantics=("parallel",),
                                             vmem_limit_bytes=FRONT_VMEM_LIMIT),
        name="front",
    )(x, g, w_all, b_gate, pos, inv, gq, gk)


def _diff_attn_kernel(qz_ref, k_ref, v_ref, lam_ref, g_ref, o_ref, m_ref, acc_ref, v1_ref,
                      *, tq, tk, seg, sub, ahead, exp_dtype, lam_init):
    i = pl.program_id(2)
    n_sub = 2 * tq // sub

    @pl.when(i == 0)
    def _():
        v1_ref[:, 0:LANES] = v_ref[...]
        v1_ref[:, LANES:] = jnp.ones((v1_ref.shape[0], LANES), v1_ref.dtype)

    m_ref[...] = jnp.full(m_ref.shape, -jnp.inf, F32)
    acc_ref[...] = jnp.zeros(acc_ref.shape, F32)

    def scores(base, item):
        r, k0, nk, _ = item
        mp, off = divmod(r * sub, tq)
        return _nt_dot(qz_ref[mp, off:off + sub, :], k_ref[pl.ds(base + k0, nk), :])

    def consume(base, item, s):
        r, k0, nk, masked = item
        off = (r * sub) % tq
        rs = slice(r * sub, (r + 1) * sub)
        if masked:
            row = lax.broadcasted_iota(jnp.int32, s.shape, 0) + off
            col = lax.broadcasted_iota(jnp.int32, s.shape, 1) + k0
            s = jnp.where((col // CHUNK) <= (row // CHUNK), s, -jnp.inf)
        m_prev = m_ref[rs]
        m_new = jnp.maximum(m_prev, jnp.max(s, axis=-1, keepdims=True))
        alpha = jnp.exp2(m_prev - m_new)
        x = s - jnp.concatenate([m_new] * (nk // LANES), axis=1)
        p = jnp.exp2(x.astype(exp_dtype)).astype(BF16)
        v1 = v1_ref[pl.ds(base + k0, nk), :]
        acc_ref[rs] = jnp.concatenate([alpha, alpha], axis=1) * acc_ref[rs] + _dot(p, v1)
        m_ref[rs] = m_new

    def pipelined(base, items):
        pending = [scores(base, it) for it in items[:ahead]]
        for n, it in enumerate(items):
            s = pending.pop(0)
            if n + ahead < len(items):
                pending.append(scores(base, items[n + ahead]))
            consume(base, it, s)

    full_items = [(r, kb * seg, seg, False) for kb in range(tk // seg) for r in range(n_sub)]

    def body(j, carry):
        pipelined(pl.multiple_of(j * tk, tk), full_items)
        return carry

    lax.fori_loop(0, i * (tq // tk), body, 0)

    diag_items = []
    for kb in range(tq // seg):
        for r in range(n_sub):
            off = (r * sub) % tq
            k0, k1 = kb * seg, min((kb + 1) * seg, off + sub)
            if k1 > k0:
                diag_items.append((r, k0, k1 - k0, k1 > off))
    pipelined(pl.multiple_of(i * tq, tq), diag_items)

    o = acc_ref[:, 0:LANES] / acc_ref[:, LANES:]
    lp = lam_ref[...]
    lam = (jnp.exp(jnp.sum(lp[0:1] * lp[1:2], axis=-1, keepdims=True))
           - jnp.exp(jnp.sum(lp[2:3] * lp[3:4], axis=-1, keepdims=True)) + lam_init)
    d = o[:tq] - lam * o[tq:]
    ms = jnp.mean(d * d, axis=-1, keepdims=True)
    o_ref[...] = (d * lax.rsqrt(ms + EPS) * g_ref[...] * (1.0 - lam_init)).astype(o_ref.dtype)


def _diff_attn(qz, kr, proj, lam_pack, g, *, batch, seq, tq, tk, seg, sub, ahead, exp_dtype, lam_init):
    t = batch * seq
    nq = seq // tq
    return pl.pallas_call(
        functools.partial(_diff_attn_kernel, tq=tq, tk=tk, seg=seg, sub=sub, ahead=ahead,
                          exp_dtype=exp_dtype, lam_init=lam_init),
        grid=(batch, DA_HEADS, nq),
        in_specs=[
            pl.BlockSpec((2, tq, LANES), lambda b, h, i: (0, b * nq + i, h)),
            pl.BlockSpec((seq, LANES), lambda b, h, i: (b, h)),
            pl.BlockSpec((seq, LANES), lambda b, h, i: (b, COL_DV // LANES + h)),
            pl.BlockSpec((8, LANES), lambda b, h, i: (0, 0)),
            pl.BlockSpec((1, LANES), lambda b, h, i: (0, 0)),
        ],
        out_specs=pl.BlockSpec((tq, LANES), lambda b, h, i: (b * nq + i, h)),
        out_shape=jax.ShapeDtypeStruct((t, HW), BF16),
        scratch_shapes=[
            pltpu.VMEM((2 * tq, LANES), F32),
            pltpu.VMEM((2 * tq, 2 * LANES), F32),
            pltpu.VMEM((seq, 2 * LANES), BF16),
        ],
        compiler_params=_cparams(("parallel", "parallel", "arbitrary")),
        name="diff_attn",
    )(qz, kr, proj, lam_pack, g)


def _gdn_prep_kernel(q_ref, k_ref, v_ref, qh_ref, kh_ref, vh_ref, ab_ref, wc_ref, alog_ref, dtb_ref,
                     pq_ref, c_ref, cd_ref, op_ref,
                     xp_ref, qn_ref, kn_ref, vn_ref, beta_ref, g_ref, grow_ref, *, ts, cpb):
    first = pl.program_id(1) == 0
    nc = ts // CHUNK

    for s, (cur_ref, halo_ref, dst_ref) in enumerate(
            ((q_ref, qh_ref, qn_ref), (k_ref, kh_ref, kn_ref), (v_ref, vh_ref, vn_ref))):
        halo = halo_ref[...].astype(F32)
        xp_ref[0:8, :] = jnp.where(first, 0.0, halo)
        xp_ref[8:, :] = cur_ref[...].astype(F32)
        acc = xp_ref[8:8 + ts, :] * wc_ref[CONV_K - 1:CONV_K, s * HW:(s + 1) * HW]
        for j in range(1, CONV_K):
            acc = acc + xp_ref[8 - j:8 - j + ts, :] * wc_ref[CONV_K - 1 - j:CONV_K - j, s * HW:(s + 1) * HW]
        act = acc * jax.nn.sigmoid(acc)
        if s < 2:
            for h in range(GDN_HEADS):
                sl = slice(h * LANES, (h + 1) * LANES)
                a_h = act[:, sl]
                dst_ref[:, sl] = a_h * lax.rsqrt(jnp.sum(a_h * a_h, axis=-1, keepdims=True) + EPS)
        else:
            dst_ref[...] = act

    ab = ab_ref[...]
    beta_ref[...] = jax.nn.sigmoid(ab)
    xa = ab + dtb_ref[...]
    softplus = jnp.maximum(xa, 0.0) + jnp.log1p(jnp.exp(-jnp.abs(xa)))
    la = -jnp.exp(alog_ref[...]) * softplus
    ti = lax.broadcasted_iota(jnp.int32, (ts, ts), 0)
    tj = lax.broadcasted_iota(jnp.int32, (ts, ts), 1)
    ltri = jnp.where((tj <= ti) & ((ti // CHUNK) == (tj // CHUNK)), 1.0, 0.0).astype(BF16)
    g_all = sum(_dot(ltri, part) for part in _split_bf16(la))
    g_ref[...] = g_all
    sel_r = lax.broadcasted_iota(jnp.int32, (8, LANES), 0)
    sel_c = lax.broadcasted_iota(jnp.int32, (8, LANES), 1)
    sel = jnp.where(sel_c == sel_r + AB_A0, 1.0, 0.0).astype(BF16)
    grow_all = sum(_nt_dot(sel, part) for part in _split_bf16(g_all))
    for c in range(nc):
        grow_ref[c] = grow_all[:, c * CHUNK:(c + 1) * CHUNK]

    ri = lax.broadcasted_iota(jnp.int32, (CHUNK, CHUNK), 0)
    ci = lax.broadcasted_iota(jnp.int32, (CHUNK, CHUNK), 1)
    incl = ci <= ri
    strict = ci < ri
    scale = GDN_K_DIM ** -0.5

    eye = jnp.where(ri == ci, 1.0, 0.0)
    same_blk = (ri // 16) == (ci // 16)

    def bdot(a, b):
        return _dot(a.astype(BF16), b.astype(BF16))

    def group_body(cg, carry):
        us = []
        for cc in range(cpb):
            c = cg * cpb + cc
            r0 = pl.multiple_of(c * CHUNK, CHUNK)
            rows = pl.ds(r0, CHUNK)
            for h in range(GDN_HEADS):
                sl = slice(h * LANES, (h + 1) * LANES)
                gcol = g_ref[rows, AB_A0 + h:AB_A0 + h + 1]
                glast = g_ref[pl.ds(r0 + CHUNK - 1, 1), AB_A0 + h:AB_A0 + h + 1]
                grow = grow_ref[c, h:h + 1, :]
                us.append(dict(
                    c=c, h=h, rows=rows, sl=sl, gcol=gcol, glast=glast,
                    q=qn_ref[rows, sl] * scale, k=kn_ref[rows, sl], v=vn_ref[rows, sl],
                    b=beta_ref[rows, AB_B0 + h:AB_B0 + h + 1], eg=jnp.exp(gcol),
                    decay=jnp.where(incl, jnp.exp(jnp.where(incl, gcol - grow, 0.0)), 0.0)))

        for u in us:
            kb = u["k"].astype(BF16)
            u["qk_kk"] = _nt_dot(jnp.concatenate([u["q"].astype(BF16), kb], axis=0), kb)
        for u in us:
            a = jnp.where(strict, u["b"] * u["qk_kk"][CHUNK:] * u["decay"], 0.0)
            ad = jnp.where(same_blk, a, 0.0)
            u["ao"] = a - ad
            u["x"] = eye - ad
            u["ad"] = ad
        for u in us:
            u["p"] = bdot(u["ad"], u["ad"])
        for _ in range(2):
            for u in us:
                xp = bdot(u["x"], u["p"])
                u["p"] = bdot(u["p"], u["p"])
                u["x"] = u["x"] + xp
        for u in us:
            u["x"] = u["x"] + bdot(u["x"], u["p"])
        for u in us:
            rhs = jnp.concatenate([u["k"] * (u["b"] * u["eg"]), u["v"] * u["b"]], axis=1)
            u["bm"] = bdot(u["x"], u["ao"])
            u["z"] = bdot(u["x"], rhs)
        for u in us:
            u["b2"] = bdot(u["bm"], u["bm"])
        for u in us:
            u["z"] = u["z"] + bdot(u["b2"], u["z"])
        for u in us:
            u["wu"] = (u["z"] - bdot(u["bm"], u["z"])).astype(BF16)
        for u in us:
            qkm = (u["qk_kk"][:CHUNK] * u["decay"]).astype(BF16)
            kd = (u["k"] * jnp.exp(u["glast"] - u["gcol"])).astype(BF16)
            u["r2"] = _dot(qkm, u["wu"])
            u["r3"] = _tn_dot(kd, u["wu"])
        for u in us:
            c, h = u["c"], u["h"]
            pq_ref[c, h, 0:LANES, :] = u["r3"][:, :LANES].astype(pq_ref.dtype)
            pq_ref[c, h, LANES:, :] = (u["q"] * u["eg"] - u["r2"][:, :LANES]).astype(pq_ref.dtype)
            c_ref[c, h] = u["r3"][:, LANES:].astype(c_ref.dtype)
            cd_ref[c, h] = jnp.broadcast_to(jnp.exp(u["glast"]), (8, LANES))
            op_ref[u["rows"], u["sl"]] = u["r2"][:, LANES:].astype(op_ref.dtype)
        return carry

    lax.fori_loop(0, nc // cpb, group_body, 0)


def _gdn_prep(proj, ab, wconv, alog_row, dtb_row, *, batch, seq, ts, cpb):
    t = batch * seq
    nt = seq // ts
    nc = ts // CHUNK
    n_chunks = t // CHUNK
    hb = ts // 8

    def cur(col):
        return pl.BlockSpec((ts, HW), lambda b, i: (b * nt + i, col // HW))

    def halo(col):
        return pl.BlockSpec((8, HW), lambda b, i: (jnp.maximum((b * nt + i) * hb - 1, 0), col // HW))

    return pl.pallas_call(
        functools.partial(_gdn_prep_kernel, ts=ts, cpb=cpb),
        grid=(batch, nt),
        in_specs=[
            cur(COL_GQ), cur(COL_GK), cur(COL_GV), halo(COL_GQ), halo(COL_GK), halo(COL_GV),
            pl.BlockSpec((ts, LANES), lambda b, i: (b * nt + i, 0)),
            pl.BlockSpec((CONV_K, 3 * HW), lambda b, i: (0, 0)),
            pl.BlockSpec((1, LANES), lambda b, i: (0, 0)),
            pl.BlockSpec((1, LANES), lambda b, i: (0, 0)),
        ],
        out_specs=[
            pl.BlockSpec((nc, GDN_HEADS, GDN_K_DIM + CHUNK, LANES), lambda b, i: (b * nt + i, 0, 0, 0)),
            pl.BlockSpec((nc, GDN_HEADS, GDN_K_DIM, LANES), lambda b, i: (b * nt + i, 0, 0, 0)),
            pl.BlockSpec((nc, GDN_HEADS, 8, LANES), lambda b, i: (b * nt + i, 0, 0, 0)),
            pl.BlockSpec((ts, HW), lambda b, i: (b * nt + i, 0)),
        ],
        out_shape=[
            jax.ShapeDtypeStruct((n_chunks, GDN_HEADS, GDN_K_DIM + CHUNK, LANES), BF16),
            jax.ShapeDtypeStruct((n_chunks, GDN_HEADS, GDN_K_DIM, LANES), BF16),
            jax.ShapeDtypeStruct((n_chunks, GDN_HEADS, 8, LANES), F32),
            jax.ShapeDtypeStruct((t, HW), BF16),
        ],
        scratch_shapes=[
            pltpu.VMEM((ts + 8, HW), F32),
            pltpu.VMEM((ts, HW), F32),
            pltpu.VMEM((ts, HW), F32),
            pltpu.VMEM((ts, HW), F32),
            pltpu.VMEM((ts, LANES), F32),
            pltpu.VMEM((ts, LANES), F32),
            pltpu.VMEM((nc, 8, CHUNK), F32),
        ],
        compiler_params=_cparams(("parallel", "parallel")),
        name="gdn_prep",
    )(proj, proj, proj, proj, proj, proj, ab, wconv, alog_row, dtb_row)


def _gdn_scan_kernel(pq_ref, c_ref, cd_ref, op_ref, z_ref, gn_ref, o_ref, s_ref, *, ts):
    @pl.when(pl.program_id(1) == 0)
    def _():
        s_ref[...] = jnp.zeros(s_ref.shape, F32)

    nc = ts // CHUNK

    def chunk_body(c, carry):
        r0 = pl.multiple_of(c * CHUNK, CHUNK)
        rows = pl.ds(r0, CHUNK)
        for h in range(GDN_HEADS):
            sl = slice(h * LANES, (h + 1) * LANES)
            s = s_ref[h]
            r = _dot(pq_ref[c, h], s.astype(BF16))
            o = r[GDN_K_DIM:] + op_ref[rows, sl].astype(F32)
            s_ref[h] = cd_ref[c, h, 0:1, :] * s - r[:GDN_K_DIM] + c_ref[c, h].astype(F32)
            on = o * lax.rsqrt(jnp.mean(o * o, axis=-1, keepdims=True) + EPS) * gn_ref[...]
            z = z_ref[rows, sl].astype(F32)
            o_ref[rows, sl] = (on * (z * jax.nn.sigmoid(z))).astype(o_ref.dtype)
        return carry

    lax.fori_loop(0, nc, chunk_body, 0)


def _gdn_scan(pq, cmat, cd, op, proj, gn, *, batch, seq, ts):
    t = batch * seq
    nt = seq // ts
    nc = ts // CHUNK
    return pl.pallas_call(
        functools.partial(_gdn_scan_kernel, ts=ts),
        grid=(batch, nt),
        in_specs=[
            pl.BlockSpec((nc, GDN_HEADS, GDN_K_DIM + CHUNK, LANES), lambda b, i: (b * nt + i, 0, 0, 0)),
            pl.BlockSpec((nc, GDN_HEADS, GDN_K_DIM, LANES), lambda b, i: (b * nt + i, 0, 0, 0)),
            pl.BlockSpec((nc, GDN_HEADS, 8, LANES), lambda b, i: (b * nt + i, 0, 0, 0)),
            pl.BlockSpec((ts, HW), lambda b, i: (b * nt + i, 0)),
            pl.BlockSpec((ts, HW), lambda b, i: (b * nt + i, COL_GZ // HW)),
            pl.BlockSpec((1, LANES), lambda b, i: (0, 0)),
        ],
        out_specs=pl.BlockSpec((ts, HW), lambda b, i: (b * nt + i, 0)),
        out_shape=jax.ShapeDtypeStruct((t, HW), BF16),
        scratch_shapes=[pltpu.VMEM((GDN_HEADS, GDN_K_DIM, GDN_V_DIM), F32)],
        compiler_params=_cparams(("parallel", "arbitrary")),
        name="gdn_scan",
    )(pq, cmat, cd, op, proj, gn)


def _mem_kv_kernel(m_ref, g_ref, w_ref, gk_ref, k_ref, v_ref):
    m = m_ref[...]
    mn = m * lax.rsqrt(jnp.mean(m * m, axis=-1, keepdims=True) + EPS) * g_ref[...]
    kv = _dot(mn.astype(BF16), w_ref[...])
    for h in range(CA_HEADS):
        sl = slice(h * LANES, (h + 1) * LANES)
        kh = kv[:, sl]
        k_ref[:, sl] = (kh * lax.rsqrt(jnp.mean(kh * kh, axis=-1, keepdims=True) + EPS)
                        * gk_ref[...]).astype(k_ref.dtype)
    v_ref[...] = kv[:, CA_Q:].astype(v_ref.dtype)


def _mem_kv(mem2, g, w, gk, *, batch, n_mem):
    d = mem2.shape[1]
    return pl.pallas_call(
        _mem_kv_kernel,
        grid=(batch,),
        in_specs=[
            pl.BlockSpec((n_mem, d), lambda b: (b, 0)),
            pl.BlockSpec((1, d), lambda b: (0, 0)),
            pl.BlockSpec((d, 2 * CA_Q), lambda b: (0, 0)),
            pl.BlockSpec((1, LANES), lambda b: (0, 0)),
        ],
        out_specs=[
            pl.BlockSpec((n_mem, CA_Q), lambda b: (b, 0)),
            pl.BlockSpec((n_mem, CA_Q), lambda b: (b, 0)),
        ],
        out_shape=[
            jax.ShapeDtypeStruct((batch * n_mem, CA_Q), BF16),
            jax.ShapeDtypeStruct((batch * n_mem, CA_Q), BF16),
        ],
        compiler_params=_cparams(("parallel",)),
        name="mem_kv",
    )(mem2, g, w, gk)


def _back_kernel(x_ref, oa_ref, ob_ref, cq_ref, gt_ref, kc_ref, vc_ref, gcq_ref,
                 wa_ref, wb_ref, wc_ref, wo_ref, g2_ref, w1_ref, w2_ref, o_ref, *, tf):
    d = x_ref.shape[1]
    heads = []
    for h in range(CA_HEADS):
        sl = slice(h * LANES, (h + 1) * LANES)
        q = cq_ref[:, sl].astype(F32)
        qn = (q * lax.rsqrt(jnp.mean(q * q, axis=-1, keepdims=True) + EPS) * gcq_ref[...]
              * (CA_DIM ** -0.5))
        s = _nt_dot(qn.astype(BF16), kc_ref[:, sl])
        p = jnp.exp(s - jnp.max(s, axis=-1, keepdims=True))
        l = jnp.sum(p, axis=-1, keepdims=True)
        heads.append((_dot(p.astype(BF16), vc_ref[:, sl]) / l).astype(BF16))
    oc = jnp.concatenate(heads, axis=1)
    ya = _dot(oa_ref[...], wa_ref[...])
    yb = _dot(ob_ref[...], wb_ref[...])
    yc = _dot(oc, wc_ref[...])
    mixed = (gt_ref[:, 0:d].astype(F32) * ya + gt_ref[:, d:2 * d].astype(F32) * yb
             + gt_ref[:, 2 * d:3 * d].astype(F32) * yc)
    x1 = x_ref[...] + _dot(mixed.astype(BF16), wo_ref[...])
    ms = jnp.mean(x1 * x1, axis=-1, keepdims=True)
    h2 = (x1 * lax.rsqrt(ms + EPS) * g2_ref[...]).astype(BF16)
    o_ref[...] = x1
    for c0 in range(0, w1_ref.shape[1], tf):
        a = jnp.maximum(_dot(h2, w1_ref[:, c0:c0 + tf]), 0.0)
        o_ref[...] += _dot((a * a).astype(BF16), w2_ref[c0:c0 + tf, :])


def _back(x, oa, ob, proj, gates, kc, vc, gcq, wa, wb, wc, wo, g2, w1, w2, *, seq, n_mem, tm, tf):
    t, d = x.shape
    tiles_per_seq = seq // tm

    def tok(n, col=0):
        return pl.BlockSpec((tm, n), lambda i: (i, col))

    def const(a):
        return pl.BlockSpec(a.shape, lambda i: (0, 0), pipeline_mode=pl.Buffered(1))

    def mem_blk():
        return pl.BlockSpec((n_mem, CA_Q), lambda i: (i // tiles_per_seq, 0))

    return pl.pallas_call(
        functools.partial(_back_kernel, tf=tf),
        grid=(t // tm,),
        in_specs=[
            tok(d), tok(HW), tok(HW), tok(CA_Q, COL_CQ // CA_Q), tok(N_BRANCH * d),
            mem_blk(), mem_blk(), const(gcq), const(wa), const(wb), const(wc), const(wo), const(g2),
            const(w1), const(w2),
        ],
        out_specs=tok(d),
        out_shape=jax.ShapeDtypeStruct((t, d), F32),
        compiler_params=pltpu.CompilerParams(dimension_semantics=("parallel",),
                                             vmem_limit_bytes=BACK_VMEM_LIMIT),
        name="back",
    )(x, oa, ob, proj, gates, kc, vc, gcq, wa, wb, wc, wo, g2, w1, w2)


def _pick(n, pref):
    t = min(pref, n)
    while n % t:
        t //= 2
    return t


def _row(v, width=None):
    v = v.astype(F32).reshape(1, -1)
    if width is not None and v.shape[1] < width:
        v = jnp.pad(v, ((0, 0), (0, width - v.shape[1])))
    return v


def _layer(x2, mem2, pos2, inv_row, batch, seq, n_mem, l, p):
    t, d = x2.shape
    (ln1_g, w_in, w_gate, b_gate, da_qnorm_g, da_knorm_g, lq1, lk1, lq2, lk2, da_subln_g, w_o_diff,
     w_conv, gdn_a_log, gdn_dt_bias, gdn_norm_g, w_o_delta, mem_norm_g, w_mem_kv, ca_qnorm_g,
     ca_knorm_g, w_o_cross, w_out, ln2_g, w_mlp1, w_mlp2) = p

    o = [0]
    for n in (HW, HW, HW, HW, HW, HW, HW, GDN_HEADS, GDN_HEADS, CA_Q):
        o.append(o[-1] + n)
    sec = [w_in[:, o[i]:o[i + 1]] for i in range(10)]
    w_ab = jnp.zeros((d, MXU_N), F32)
    w_ab = w_ab.at[:, AB_B0:AB_B0 + GDN_HEADS].set(sec[7]).at[:, AB_A0:AB_A0 + GDN_HEADS].set(sec[8])
    w_all = jnp.concatenate([sec[3], sec[4], sec[5], sec[6], sec[9], sec[2], sec[0], sec[1], w_ab, w_gate],
                            axis=1).astype(BF16)
    gq = _row(jnp.tile(da_qnorm_g, 2))
    gk = _row(jnp.tile(da_knorm_g, 2))
    proj, ab, gates, qz, kr = _front(x2, _row(ln1_g), w_all, _row(b_gate), pos2, inv_row, gq, gk,
                                     tm=_pick(t, 512))

    lam_pack = jnp.zeros((8, LANES), F32)
    lam_pack = lam_pack.at[0:4, 0:DA_QK_DIM].set(jnp.stack([lq1, lk1, lq2, lk2]).astype(F32))
    lam_init = 0.8 - 0.6 * math.exp(-0.3 * l)
    oa = _diff_attn(qz, kr, proj, lam_pack, _row(da_subln_g), batch=batch, seq=seq,
                    tq=_pick(seq, 1024), tk=_pick(seq, 1024), seg=_pick(seq, 512), sub=256, ahead=2,
                    exp_dtype=BF16, lam_init=lam_init)

    alog_row = jnp.zeros((1, LANES), F32).at[0, AB_A0:AB_A0 + GDN_HEADS].set(gdn_a_log.astype(F32))
    dtb_row = jnp.zeros((1, LANES), F32).at[0, AB_A0:AB_A0 + GDN_HEADS].set(gdn_dt_bias.astype(F32))
    pq, cmat, cd, op = _gdn_prep(proj, ab, w_conv.astype(F32), alog_row, dtb_row,
                                 batch=batch, seq=seq, ts=_pick(seq, 256), cpb=4)
    ob = _gdn_scan(pq, cmat, cd, op, proj, _row(gdn_norm_g), batch=batch, seq=seq, ts=_pick(seq, 512))

    kc, vc = _mem_kv(mem2, _row(mem_norm_g), w_mem_kv.astype(BF16), _row(ca_knorm_g),
                     batch=batch, n_mem=n_mem)

    return _back(x2, oa, ob, proj, gates, kc, vc, _row(ca_qnorm_g), w_o_diff.astype(BF16),
                 w_o_delta.astype(BF16), w_o_cross.astype(BF16), w_out.astype(BF16), _row(ln2_g),
                 w_mlp1.astype(BF16), w_mlp2.astype(BF16), seq=seq, n_mem=n_mem,
                 tm=_pick(seq, 512), tf=1024)


def kernel(x, mem, positions, ln1_g, w_in, w_gate, b_gate, da_qnorm_g, da_knorm_g, da_lambda_q1, da_lambda_k1, da_lambda_q2, da_lambda_k2, da_subln_g, w_o_diff, w_conv, gdn_a_log, gdn_dt_bias, gdn_norm_g, w_o_delta, mem_norm_g, w_mem_kv, ca_qnorm_g, ca_knorm_g, w_o_cross, w_out, ln2_g, w_mlp1, w_mlp2):
    batch, seq, d = x.shape
    n_mem = mem.shape[1]
    x2 = x.reshape(batch * seq, d)
    mem2 = mem.reshape(batch * n_mem, d)
    pos2 = jnp.broadcast_to(positions.reshape(batch * seq, 1), (batch * seq, LANES))
    half = DA_QK_DIM // 2
    inv = jnp.exp(-math.log(ROPE_THETA) * jnp.arange(half, dtype=F32) / half)
    inv_row = jnp.tile(inv, LANES // half).reshape(1, LANES)
    params = (ln1_g, w_in, w_gate, b_gate, da_qnorm_g, da_knorm_g, da_lambda_q1, da_lambda_k1,
              da_lambda_q2, da_lambda_k2, da_subln_g, w_o_diff, w_conv, gdn_a_log, gdn_dt_bias,
              gdn_norm_g, w_o_delta, mem_norm_g, w_mem_kv, ca_qnorm_g, ca_knorm_g, w_o_cross, w_out,
              ln2_g, w_mlp1, w_mlp2)
    for l in range(ln1_g.shape[0]):
        x2 = _layer(x2, mem2, pos2, inv_row, batch, seq, n_mem, l, tuple(a[l] for a in params))
    return x2.reshape(batch, seq, d)
```

```python
import functools
import math

import jax
import jax.numpy as jnp
from jax import lax
from jax.experimental import pallas as pl
from jax.experimental.pallas import tpu as pltpu

F32 = jnp.float32
BF16 = jnp.bfloat16

EPS = 1e-6
CHUNK = 64
ROPE_THETA = 10000.0
N_BRANCH = 3

DA_HEADS = 6
DA_QK_DIM = 64
DA_V_DIM = 128
GDN_HEADS = 6
GDN_K_DIM = 128
GDN_V_DIM = 128
CONV_K = 4
CA_HEADS = 4
CA_DIM = 128

LANES = 128
HW = GDN_HEADS * GDN_K_DIM
CA_Q = CA_HEADS * CA_DIM

MXU_N = 256

COL_GQ, COL_GK, COL_GV, COL_GZ, COL_CQ, COL_DV = 0, HW, 2 * HW, 3 * HW, 4 * HW, 4 * HW + CA_Q
N_PROJ = 5 * HW + CA_Q
WCOL_DQ, WCOL_DK, WCOL_AB = N_PROJ, N_PROJ + HW, N_PROJ + 2 * HW
WCOL_GATE = WCOL_AB + MXU_N
AB_B0, AB_A0 = 0, 8

VMEM_LIMIT = 48 * 1024 * 1024
FRONT_VMEM_LIMIT = 56 * 1024 * 1024
BACK_VMEM_LIMIT = 58 * 1024 * 1024


def _cparams(sem):
    return pltpu.CompilerParams(dimension_semantics=sem, vmem_limit_bytes=VMEM_LIMIT)


def _nt_dot(a, b):
    return lax.dot_general(a, b, (((1,), (1,)), ((), ())), preferred_element_type=F32)


def _tn_dot(a, b):
    return lax.dot_general(a, b, (((0,), (0,)), ((), ())), preferred_element_type=F32)


def _dot(a, b):
    return jnp.dot(a, b, preferred_element_type=F32)


def _split_bf16(x):
    hi = x.astype(BF16)
    r = x - hi.astype(F32)
    mid = r.astype(BF16)
    lo = (r - mid.astype(F32)).astype(BF16)
    return hi, mid, lo


def _front_kernel(x_ref, g_ref, w_ref, bg_ref, pos_ref, inv_ref, gq_ref, gk_ref,
                  proj_ref, ab_ref, gates_ref, qz_ref, kr_ref):
    tm = x_ref.shape[0]
    x = x_ref[...]
    h = (x * lax.rsqrt(jnp.mean(x * x, axis=-1, keepdims=True) + EPS) * g_ref[...]).astype(BF16)

    def cols(c0, n=MXU_N):
        return _dot(h, w_ref[:, c0:c0 + n])

    ang = pos_ref[...].astype(F32) * inv_ref[...]
    cos = jnp.cos(ang)
    sin = jnp.sin(ang)
    lane = lax.broadcasted_iota(jnp.int32, (tm, LANES), 1)
    lo_map = lane < DA_QK_DIM
    first_half = (lane % DA_QK_DIM) < (DA_QK_DIM // 2)
    sin_signed = jnp.where(first_half, -sin, sin)
    scale = DA_QK_DIM ** -0.5 * math.log2(math.e)

    def norm_rope(x, g):
        s = x * x
        tot = jnp.sum(s, axis=-1, keepdims=True)
        lo = jnp.sum(jnp.where(lo_map, s, 0.0), axis=-1, keepdims=True)
        ms = jnp.where(lo_map, lo, tot - lo) * (1.0 / DA_QK_DIM)
        y = x * lax.rsqrt(ms + EPS) * g
        partner = jnp.where(first_half,
                            pltpu.roll(y, LANES - DA_QK_DIM // 2, axis=1),
                            pltpu.roll(y, DA_QK_DIM // 2, axis=1))
        return y * cos + partner * sin_signed

    for c0 in range(0, HW, MXU_N):
        yq = cols(WCOL_DQ + c0)
        yk = cols(WCOL_DK + c0)
        for j in range(MXU_N // LANES):
            sl = slice(c0 + j * LANES, c0 + (j + 1) * LANES)
            q = norm_rope(yq[:, j * LANES:(j + 1) * LANES], gq_ref[...]) * scale
            qz_ref[0, :, sl] = jnp.where(lo_map, q, 0.0).astype(qz_ref.dtype)
            qz_ref[1, :, sl] = jnp.where(lo_map, 0.0, q).astype(qz_ref.dtype)
            k = norm_rope(yk[:, j * LANES:(j + 1) * LANES], gk_ref[...])
            kr_ref[:, sl] = k.astype(kr_ref.dtype)

    ab_ref[...] = cols(WCOL_AB)[:, :LANES]

    for c0 in range(0, gates_ref.shape[1], MXU_N):
        y = cols(WCOL_GATE + c0) + bg_ref[:, c0:c0 + MXU_N]
        gates_ref[:, c0:c0 + MXU_N] = jax.nn.sigmoid(y).astype(gates_ref.dtype)

    for c0 in range(0, N_PROJ, MXU_N):
        proj_ref[:, c0:c0 + MXU_N] = cols(c0).astype(proj_ref.dtype)


def _front(x, g, w_all, b_gate, pos, inv, gq, gk, *, tm):
    t, d = x.shape
    n_gate = b_gate.shape[1]

    def tok(n):
        return pl.BlockSpec((tm, n), lambda i: (i, 0))

    def const(shape, **kw):
        return pl.BlockSpec(shape, lambda i: (0,) * len(shape), **kw)

    return pl.pallas_call(
        _front_kernel,
        grid=(t // tm,),
        in_specs=[
            tok(d), const((1, d)),
            const(w_all.shape, pipeline_mode=pl.Buffered(1)),
            const((1, n_gate)), tok(1), const((1, LANES)), const((1, LANES)), const((1, LANES)),
        ],
        out_specs=[
            tok(N_PROJ), tok(LANES), tok(n_gate),
            pl.BlockSpec((2, tm, HW), lambda i: (0, i, 0)),
            tok(HW),
        ],
        out_shape=[
            jax.ShapeDtypeStruct((t, N_PROJ), BF16),
            jax.ShapeDtypeStruct((t, LANES), F32),
            jax.ShapeDtypeStruct((t, n_gate), BF16),
            jax.ShapeDtypeStruct((2, t, HW), BF16),
            jax.ShapeDtypeStruct((t, HW), BF16),
        ],
        compiler_params=pltpu.CompilerParams(dimension_semantics=("parallel",),
                                             vmem_limit_bytes=FRONT_VMEM_LIMIT),
        name="front",
    )(x, g, w_all, b_gate, pos, inv, gq, gk)


def _diff_attn_kernel(qz_ref, k_ref, v_ref, lam_ref, g_ref, o_ref, m_ref, acc_ref, v1_ref,
                      *, tq, tk, seg, sub, ahead, exp_dtype, lam_init):
    i = pl.program_id(2)
    n_sub = 2 * tq // sub

    @pl.when(i == 0)
    def _():
        v1_ref[:, 0:LANES] = v_ref[...]
        v1_ref[:, LANES:] = jnp.ones((v1_ref.shape[0], LANES), v1_ref.dtype)

    m_ref[...] = jnp.full(m_ref.shape, -jnp.inf, F32)
    acc_ref[...] = jnp.zeros(acc_ref.shape, F32)

    def scores(base, item):
        r, k0, nk, _ = item
        mp, off = divmod(r * sub, tq)
        return _nt_dot(qz_ref[mp, off:off + sub, :], k_ref[pl.ds(base + k0, nk), :])

    def consume(base, item, s):
        r, k0, nk, masked = item
        off = (r * sub) % tq
        rs = slice(r * sub, (r + 1) * sub)
        if masked:
            row = lax.broadcasted_iota(jnp.int32, s.shape, 0) + off
            col = lax.broadcasted_iota(jnp.int32, s.shape, 1) + k0
            s = jnp.where((col // CHUNK) <= (row // CHUNK), s, -jnp.inf)
        m_prev = m_ref[rs]
        m_new = jnp.maximum(m_prev, jnp.max(s, axis=-1, keepdims=True))
        alpha = jnp.exp2(m_prev - m_new)
        x = s - jnp.concatenate([m_new] * (nk // LANES), axis=1)
        p = jnp.exp2(x.astype(exp_dtype)).astype(BF16)
        v1 = v1_ref[pl.ds(base + k0, nk), :]
        acc_ref[rs] = jnp.concatenate([alpha, alpha], axis=1) * acc_ref[rs] + _dot(p, v1)
        m_ref[rs] = m_new

    def pipelined(base, items):
        pending = [scores(base, it) for it in items[:ahead]]
        for n, it in enumerate(items):
            s = pending.pop(0)
            if n + ahead < len(items):
                pending.append(scores(base, items[n + ahead]))
            consume(base, it, s)

    full_items = [(r, kb * seg, seg, False) for kb in range(tk // seg) for r in range(n_sub)]

    def body(j, carry):
        pipelined(pl.multiple_of(j * tk, tk), full_items)
        return carry

    lax.fori_loop(0, i * (tq // tk), body, 0)

    diag_items = []
    for kb in range(tq // seg):
        for r in range(n_sub):
            off = (r * sub) % tq
            k0, k1 = kb * seg, min((kb + 1) * seg, off + sub)
            if k1 > k0:
                diag_items.append((r, k0, k1 - k0, k1 > off))
    pipelined(pl.multiple_of(i * tq, tq), diag_items)

    o = acc_ref[:, 0:LANES] / acc_ref[:, LANES:]
    lp = lam_ref[...]
    lam = (jnp.exp(jnp.sum(lp[0:1] * lp[1:2], axis=-1, keepdims=True))
           - jnp.exp(jnp.sum(lp[2:3] * lp[3:4], axis=-1, keepdims=True)) + lam_init)
    d = o[:tq] - lam * o[tq:]
    ms = jnp.mean(d * d, axis=-1, keepdims=True)
    o_ref[...] = (d * lax.rsqrt(ms + EPS) * g_ref[...] * (1.0 - lam_init)).astype(o_ref.dtype)


def _diff_attn(qz, kr, proj, lam_pack, g, *, batch, seq, tq, tk, seg, sub, ahead, exp_dtype, lam_init):
    t = batch * seq
    nq = seq // tq
    return pl.pallas_call(
        functools.partial(_diff_attn_kernel, tq=tq, tk=tk, seg=seg, sub=sub, ahead=ahead,
                          exp_dtype=exp_dtype, lam_init=lam_init),
        grid=(batch, DA_HEADS, nq),
        in_specs=[
            pl.BlockSpec((2, tq, LANES), lambda b, h, i: (0, b * nq + i, h)),
            pl.BlockSpec((seq, LANES), lambda b, h, i: (b, h)),
            pl.BlockSpec((seq, LANES), lambda b, h, i: (b, COL_DV // LANES + h)),
            pl.BlockSpec((8, LANES), lambda b, h, i: (0, 0)),
            pl.BlockSpec((1, LANES), lambda b, h, i: (0, 0)),
        ],
        out_specs=pl.BlockSpec((tq, LANES), lambda b, h, i: (b * nq + i, h)),
        out_shape=jax.ShapeDtypeStruct((t, HW), BF16),
        scratch_shapes=[
            pltpu.VMEM((2 * tq, LANES), F32),
            pltpu.VMEM((2 * tq, 2 * LANES), F32),
            pltpu.VMEM((seq, 2 * LANES), BF16),
        ],
        compiler_params=_cparams(("parallel", "parallel", "arbitrary")),
        name="diff_attn",
    )(qz, kr, proj, lam_pack, g)


def _gdn_prep_kernel(q_ref, k_ref, v_ref, qh_ref, kh_ref, vh_ref, ab_ref, wc_ref, alog_ref, dtb_ref,
                     pq_ref, c_ref, cd_ref, op_ref,
                     xp_ref, qn_ref, kn_ref, vn_ref, beta_ref, g_ref, grow_ref, *, ts, cpb):
    first = pl.program_id(1) == 0
    nc = ts // CHUNK

    for s, (cur_ref, halo_ref, dst_ref) in enumerate(
            ((q_ref, qh_ref, qn_ref), (k_ref, kh_ref, kn_ref), (v_ref, vh_ref, vn_ref))):
        halo = halo_ref[...].astype(F32)
        xp_ref[0:8, :] = jnp.where(first, 0.0, halo)
        xp_ref[8:, :] = cur_ref[...].astype(F32)
        acc = xp_ref[8:8 + ts, :] * wc_ref[CONV_K - 1:CONV_K, s * HW:(s + 1) * HW]
        for j in range(1, CONV_K):
            acc = acc + xp_ref[8 - j:8 - j + ts, :] * wc_ref[CONV_K - 1 - j:CONV_K - j, s * HW:(s + 1) * HW]
        act = acc * jax.nn.sigmoid(acc)
        if s < 2:
            for h in range(GDN_HEADS):
                sl = slice(h * LANES, (h + 1) * LANES)
                a_h = act[:, sl]
                dst_ref[:, sl] = a_h * lax.rsqrt(jnp.sum(a_h * a_h, axis=-1, keepdims=True) + EPS)
        else:
            dst_ref[...] = act

    ab = ab_ref[...]
    beta_ref[...] = jax.nn.sigmoid(ab)
    xa = ab + dtb_ref[...]
    softplus = jnp.maximum(xa, 0.0) + jnp.log1p(jnp.exp(-jnp.abs(xa)))
    la = -jnp.exp(alog_ref[...]) * softplus
    ti = lax.broadcasted_iota(jnp.int32, (ts, ts), 0)
    tj = lax.broadcasted_iota(jnp.int32, (ts, ts), 1)
    ltri = jnp.where((tj <= ti) & ((ti // CHUNK) == (tj // CHUNK)), 1.0, 0.0).astype(BF16)
    g_all = sum(_dot(ltri, part) for part in _split_bf16(la))
    g_ref[...] = g_all
    sel_r = lax.broadcasted_iota(jnp.int32, (8, LANES), 0)
    sel_c = lax.broadcasted_iota(jnp.int32, (8, LANES), 1)
    sel = jnp.where(sel_c == sel_r + AB_A0, 1.0, 0.0).astype(BF16)
    grow_all = sum(_nt_dot(sel, part) for part in _split_bf16(g_all))
    for c in range(nc):
        grow_ref[c] = grow_all[:, c * CHUNK:(c + 1) * CHUNK]

    ri = lax.broadcasted_iota(jnp.int32, (CHUNK, CHUNK), 0)
    ci = lax.broadcasted_iota(jnp.int32, (CHUNK, CHUNK), 1)
    incl = ci <= ri
    strict = ci < ri
    scale = GDN_K_DIM ** -0.5

    eye = jnp.where(ri == ci, 1.0, 0.0)
    same_blk = (ri // 16) == (ci // 16)

    def bdot(a, b):
        return _dot(a.astype(BF16), b.astype(BF16))

    def group_body(cg, carry):
        us = []
        for cc in range(cpb):
            c = cg * cpb + cc
            r0 = pl.multiple_of(c * CHUNK, CHUNK)
            rows = pl.ds(r0, CHUNK)
            for h in range(GDN_HEADS):
                sl = slice(h * LANES, (h + 1) * LANES)
                gcol = g_ref[rows, AB_A0 + h:AB_A0 + h + 1]
                glast = g_ref[pl.ds(r0 + CHUNK - 1, 1), AB_A0 + h:AB_A0 + h + 1]
                grow = grow_ref[c, h:h + 1, :]
                us.append(dict(
                    c=c, h=h, rows=rows, sl=sl, gcol=gcol, glast=glast,
                    q=qn_ref[rows, sl] * scale, k=kn_ref[rows, sl], v=vn_ref[rows, sl],
                    b=beta_ref[rows, AB_B0 + h:AB_B0 + h + 1], eg=jnp.exp(gcol),
                    decay=jnp.where(incl, jnp.exp(jnp.where(incl, gcol - grow, 0.0)), 0.0)))

        for u in us:
            kb = u["k"].astype(BF16)
            u["qk_kk"] = _nt_dot(jnp.concatenate([u["q"].astype(BF16), kb], axis=0), kb)
        for u in us:
            a = jnp.where(strict, u["b"] * u["qk_kk"][CHUNK:] * u["decay"], 0.0)
            ad = jnp.where(same_blk, a, 0.0)
            u["ao"] = a - ad
            u["x"] = eye - ad
            u["ad"] = ad
        for u in us:
            u["p"] = bdot(u["ad"], u["ad"])
        for _ in range(2):
            for u in us:
                xp = bdot(u["x"], u["p"])
                u["p"] = bdot(u["p"], u["p"])
                u["x"] = u["x"] + xp
        for u in us:
            u["x"] = u["x"] + bdot(u["x"], u["p"])
        for u in us:
            rhs = jnp.concatenate([u["k"] * (u["b"] * u["eg"]), u["v"] * u["b"]], axis=1)
            u["bm"] = bdot(u["x"], u["ao"])
            u["z"] = bdot(u["x"], rhs)
        for u in us:
            u["b2"] = bdot(u["bm"], u["bm"])
        for u in us:
            u["z"] = u["z"] + bdot(u["b2"], u["z"])
        for u in us:
            u["wu"] = (u["z"] - bdot(u["bm"], u["z"])).astype(BF16)
        for u in us:
            qkm = (u["qk_kk"][:CHUNK] * u["decay"]).astype(BF16)
            kd = (u["k"] * jnp.exp(u["glast"] - u["gcol"])).astype(BF16)
            u["r2"] = _dot(qkm, u["wu"])
            u["r3"] = _tn_dot(kd, u["wu"])
        for u in us:
            c, h = u["c"], u["h"]
            pq_ref[c, h, 0:LANES, :] = u["r3"][:, :LANES].astype(pq_ref.dtype)
            pq_ref[c, h, LANES:, :] = (u["q"] * u["eg"] - u["r2"][:, :LANES]).astype(pq_ref.dtype)
            c_ref[c, h] = u["r3"][:, LANES:].astype(c_ref.dtype)
            cd_ref[c, h] = jnp.broadcast_to(jnp.exp(u["glast"]), (8, LANES))
            op_ref[u["rows"], u["sl"]] = u["r2"][:, LANES:].astype(op_ref.dtype)
        return carry

    lax.fori_loop(0, nc // cpb, group_body, 0)


def _gdn_prep(proj, ab, wconv, alog_row, dtb_row, *, batch, seq, ts, cpb):
    t = batch * seq
    nt = seq // ts
    nc = ts // CHUNK
    n_chunks = t // CHUNK
    hb = ts // 8

    def cur(col):
        return pl.BlockSpec((ts, HW), lambda b, i: (b * nt + i, col // HW))

    def halo(col):
        return pl.BlockSpec((8, HW), lambda b, i: (jnp.maximum((b * nt + i) * hb - 1, 0), col // HW))

    return pl.pallas_call(
        functools.partial(_gdn_prep_kernel, ts=ts, cpb=cpb),
        grid=(batch, nt),
        in_specs=[
            cur(COL_GQ), cur(COL_GK), cur(COL_GV), halo(COL_GQ), halo(COL_GK), halo(COL_GV),
            pl.BlockSpec((ts, LANES), lambda b, i: (b * nt + i, 0)),
            pl.BlockSpec((CONV_K, 3 * HW), lambda b, i: (0, 0)),
            pl.BlockSpec((1, LANES), lambda b, i: (0, 0)),
            pl.BlockSpec((1, LANES), lambda b, i: (0, 0)),
        ],
        out_specs=[
            pl.BlockSpec((nc, GDN_HEADS, GDN_K_DIM + CHUNK, LANES), lambda b, i: (b * nt + i, 0, 0, 0)),
            pl.BlockSpec((nc, GDN_HEADS, GDN_K_DIM, LANES), lambda b, i: (b * nt + i, 0, 0, 0)),
            pl.BlockSpec((nc, GDN_HEADS, 8, LANES), lambda b, i: (b * nt + i, 0, 0, 0)),
            pl.BlockSpec((ts, HW), lambda b, i: (b * nt + i, 0)),
        ],
        out_shape=[
            jax.ShapeDtypeStruct((n_chunks, GDN_HEADS, GDN_K_DIM + CHUNK, LANES), BF16),
            jax.ShapeDtypeStruct((n_chunks, GDN_HEADS, GDN_K_DIM, LANES), BF16),
            jax.ShapeDtypeStruct((n_chunks, GDN_HEADS, 8, LANES), F32),
            jax.ShapeDtypeStruct((t, HW), BF16),
        ],
        scratch_shapes=[
            pltpu.VMEM((ts + 8, HW), F32),
            pltpu.VMEM((ts, HW), F32),
            pltpu.VMEM((ts, HW), F32),
            pltpu.VMEM((ts, HW), F32),
            pltpu.VMEM((ts, LANES), F32),
            pltpu.VMEM((ts, LANES), F32),
            pltpu.VMEM((nc, 8, CHUNK), F32),
        ],
        compiler_params=_cparams(("parallel", "parallel")),
        name="gdn_prep",
    )(proj, proj, proj, proj, proj, proj, ab, wconv, alog_row, dtb_row)


def _gdn_scan_kernel(pq_ref, c_ref, cd_ref, op_ref, z_ref, gn_ref, o_ref, s_ref, *, ts):
    @pl.when(pl.program_id(0) == 0)
    def _():
        s_ref[...] = jnp.zeros(s_ref.shape, F32)

    nc = ts // CHUNK
    batch = s_ref.shape[0]

    def chunk_body(c, carry):
        r0 = pl.multiple_of(c * CHUNK, CHUNK)
        rows = pl.ds(r0, CHUNK)
        for b in range(batch):
            for h in range(GDN_HEADS):
                sl = slice(h * LANES, (h + 1) * LANES)
                s = s_ref[b, h]
                r = _dot(pq_ref[b, c, h], s.astype(BF16))
                o = r[GDN_K_DIM:] + op_ref[b, rows, sl].astype(F32)
                s_ref[b, h] = cd_ref[b, c, h, 0:1, :] * s - r[:GDN_K_DIM] + c_ref[b, c, h].astype(F32)
                on = o * lax.rsqrt(jnp.mean(o * o, axis=-1, keepdims=True) + EPS) * gn_ref[...]
                z = z_ref[b, rows, sl].astype(F32)
                o_ref[b, rows, sl] = (on * (z * jax.nn.sigmoid(z))).astype(o_ref.dtype)
        return carry

    lax.fori_loop(0, nc, chunk_body, 0)


def _gdn_scan(pq, cmat, cd, op, proj, gn, *, batch, seq, ts):
    nt = seq // ts
    nc = ts // CHUNK
    n = seq // CHUNK

    def per_chunk(a):
        return a.reshape((batch, n) + a.shape[1:])

    def chunk_blk(rows):
        return pl.BlockSpec((batch, nc, GDN_HEADS, rows, LANES), lambda i: (0, i, 0, 0, 0))

    def tok_blk(col):
        return pl.BlockSpec((batch, ts, HW), lambda i: (0, i, col))

    out = pl.pallas_call(
        functools.partial(_gdn_scan_kernel, ts=ts),
        grid=(nt,),
        in_specs=[
            chunk_blk(GDN_K_DIM + CHUNK), chunk_blk(GDN_K_DIM), chunk_blk(8),
            tok_blk(0), tok_blk(COL_GZ // HW),
            pl.BlockSpec((1, LANES), lambda i: (0, 0)),
        ],
        out_specs=tok_blk(0),
        out_shape=jax.ShapeDtypeStruct((batch, seq, HW), BF16),
        scratch_shapes=[pltpu.VMEM((batch, GDN_HEADS, GDN_K_DIM, GDN_V_DIM), F32)],
        compiler_params=_cparams(("arbitrary",)),
        name="gdn_scan",
    )(per_chunk(pq), per_chunk(cmat), per_chunk(cd), op.reshape(batch, seq, HW),
      proj.reshape(batch, seq, proj.shape[1]), gn)
    return out.reshape(batch * seq, HW)


def _mem_kv_kernel(m_ref, g_ref, w_ref, gk_ref, k_ref, v_ref):
    m = m_ref[...]
    mn = m * lax.rsqrt(jnp.mean(m * m, axis=-1, keepdims=True) + EPS) * g_ref[...]
    kv = _dot(mn.astype(BF16), w_ref[...])
    for h in range(CA_HEADS):
        sl = slice(h * LANES, (h + 1) * LANES)
        kh = kv[:, sl]
        k_ref[:, sl] = (kh * lax.rsqrt(jnp.mean(kh * kh, axis=-1, keepdims=True) + EPS)
                        * gk_ref[...]).astype(k_ref.dtype)
    v_ref[...] = kv[:, CA_Q:].astype(v_ref.dtype)


def _mem_kv(mem2, g, w, gk, *, batch, n_mem):
    d = mem2.shape[1]
    return pl.pallas_call(
        _mem_kv_kernel,
        grid=(batch,),
        in_specs=[
            pl.BlockSpec((n_mem, d), lambda b: (b, 0)),
            pl.BlockSpec((1, d), lambda b: (0, 0)),
            pl.BlockSpec((d, 2 * CA_Q), lambda b: (0, 0)),
            pl.BlockSpec((1, LANES), lambda b: (0, 0)),
        ],
        out_specs=[
            pl.BlockSpec((n_mem, CA_Q), lambda b: (b, 0)),
            pl.BlockSpec((n_mem, CA_Q), lambda b: (b, 0)),
        ],
        out_shape=[
            jax.ShapeDtypeStruct((batch * n_mem, CA_Q), BF16),
            jax.ShapeDtypeStruct((batch * n_mem, CA_Q), BF16),
        ],
        compiler_params=_cparams(("parallel",)),
        name="mem_kv",
    )(mem2, g, w, gk)


def _back_kernel(x_ref, oa_ref, ob_ref, cq_ref, gt_ref, kc_ref, vc_ref, gcq_ref,
                 wa_ref, wb_ref, wc_ref, wo_ref, g2_ref, w1_ref, w2_ref, o_ref, *, tf):
    d = x_ref.shape[1]
    heads = []
    for h in range(CA_HEADS):
        sl = slice(h * LANES, (h + 1) * LANES)
        q = cq_ref[:, sl].astype(F32)
        qn = (q * lax.rsqrt(jnp.mean(q * q, axis=-1, keepdims=True) + EPS) * gcq_ref[...]
              * (CA_DIM ** -0.5))
        s = _nt_dot(qn.astype(BF16), kc_ref[:, sl])
        p = jnp.exp(s - jnp.max(s, axis=-1, keepdims=True))
        l = jnp.sum(p, axis=-1, keepdims=True)
        heads.append((_dot(p.astype(BF16), vc_ref[:, sl]) / l).astype(BF16))
    oc = jnp.concatenate(heads, axis=1)
    ya = _dot(oa_ref[...], wa_ref[...])
    yb = _dot(ob_ref[...], wb_ref[...])
    yc = _dot(oc, wc_ref[...])
    mixed = (gt_ref[:, 0:d].astype(F32) * ya + gt_ref[:, d:2 * d].astype(F32) * yb
             + gt_ref[:, 2 * d:3 * d].astype(F32) * yc)
    x1 = x_ref[...] + _dot(mixed.astype(BF16), wo_ref[...])
    ms = jnp.mean(x1 * x1, axis=-1, keepdims=True)
    h2 = (x1 * lax.rsqrt(ms + EPS) * g2_ref[...]).astype(BF16)
    o_ref[...] = x1
    for c0 in range(0, w1_ref.shape[1], tf):
        a = jnp.maximum(_dot(h2, w1_ref[:, c0:c0 + tf]), 0.0)
        o_ref[...] += _dot((a * a).astype(BF16), w2_ref[c0:c0 + tf, :])


def _back(x, oa, ob, proj, gates, kc, vc, gcq, wa, wb, wc, wo, g2, w1, w2, *, seq, n_mem, tm, tf):
    t, d = x.shape
    tiles_per_seq = seq // tm

    def tok(n, col=0):
        return pl.BlockSpec((tm, n), lambda i: (i, col))

    def const(a):
        return pl.BlockSpec(a.shape, lambda i: (0, 0), pipeline_mode=pl.Buffered(1))

    def mem_blk():
        return pl.BlockSpec((n_mem, CA_Q), lambda i: (i // tiles_per_seq, 0))

    return pl.pallas_call(
        functools.partial(_back_kernel, tf=tf),
        grid=(t // tm,),
        in_specs=[
            tok(d), tok(HW), tok(HW), tok(CA_Q, COL_CQ // CA_Q), tok(N_BRANCH * d),
            mem_blk(), mem_blk(), const(gcq), const(wa), const(wb), const(wc), const(wo), const(g2),
            const(w1), const(w2),
        ],
        out_specs=tok(d),
        out_shape=jax.ShapeDtypeStruct((t, d), F32),
        compiler_params=pltpu.CompilerParams(dimension_semantics=("parallel",),
                                             vmem_limit_bytes=BACK_VMEM_LIMIT),
        name="back",
    )(x, oa, ob, proj, gates, kc, vc, gcq, wa, wb, wc, wo, g2, w1, w2)


def _pick(n, pref):
    t = min(pref, n)
    while n % t:
        t //= 2
    return t


def _row(v, width=None):
    v = v.astype(F32).reshape(1, -1)
    if width is not None and v.shape[1] < width:
        v = jnp.pad(v, ((0, 0), (0, width - v.shape[1])))
    return v


def _layer(x2, mem2, pos2, inv_row, batch, seq, n_mem, l, p):
    t, d = x2.shape
    (ln1_g, w_in, w_gate, b_gate, da_qnorm_g, da_knorm_g, lq1, lk1, lq2, lk2, da_subln_g, w_o_diff,
     w_conv, gdn_a_log, gdn_dt_bias, gdn_norm_g, w_o_delta, mem_norm_g, w_mem_kv, ca_qnorm_g,
     ca_knorm_g, w_o_cross, w_out, ln2_g, w_mlp1, w_mlp2) = p

    o = [0]
    for n in (HW, HW, HW, HW, HW, HW, HW, GDN_HEADS, GDN_HEADS, CA_Q):
        o.append(o[-1] + n)
    sec = [w_in[:, o[i]:o[i + 1]] for i in range(10)]
    w_ab = jnp.zeros((d, MXU_N), F32)
    w_ab = w_ab.at[:, AB_B0:AB_B0 + GDN_HEADS].set(sec[7]).at[:, AB_A0:AB_A0 + GDN_HEADS].set(sec[8])
    w_all = jnp.concatenate([sec[3], sec[4], sec[5], sec[6], sec[9], sec[2], sec[0], sec[1], w_ab, w_gate],
                            axis=1).astype(BF16)
    gq = _row(jnp.tile(da_qnorm_g, 2))
    gk = _row(jnp.tile(da_knorm_g, 2))
    proj, ab, gates, qz, kr = _front(x2, _row(ln1_g), w_all, _row(b_gate), pos2, inv_row, gq, gk,
                                     tm=_pick(t, 512))

    lam_pack = jnp.zeros((8, LANES), F32)
    lam_pack = lam_pack.at[0:4, 0:DA_QK_DIM].set(jnp.stack([lq1, lk1, lq2, lk2]).astype(F32))
    lam_init = 0.8 - 0.6 * math.exp(-0.3 * l)
    oa = _diff_attn(qz, kr, proj, lam_pack, _row(da_subln_g), batch=batch, seq=seq,
                    tq=_pick(seq, 1024), tk=_pick(seq, 1024), seg=_pick(seq, 512), sub=256, ahead=2,
                    exp_dtype=BF16, lam_init=lam_init)

    alog_row = jnp.zeros((1, LANES), F32).at[0, AB_A0:AB_A0 + GDN_HEADS].set(gdn_a_log.astype(F32))
    dtb_row = jnp.zeros((1, LANES), F32).at[0, AB_A0:AB_A0 + GDN_HEADS].set(gdn_dt_bias.astype(F32))
    pq, cmat, cd, op = _gdn_prep(proj, ab, w_conv.astype(F32), alog_row, dtb_row,
                                 batch=batch, seq=seq, ts=_pick(seq, 256), cpb=4)
    ob = _gdn_scan(pq, cmat, cd, op, proj, _row(gdn_norm_g), batch=batch, seq=seq, ts=_pick(seq, 256))

    kc, vc = _mem_kv(mem2, _row(mem_norm_g), w_mem_kv.astype(BF16), _row(ca_knorm_g),
                     batch=batch, n_mem=n_mem)

    return _back(x2, oa, ob, proj, gates, kc, vc, _row(ca_qnorm_g), w_o_diff.astype(BF16),
                 w_o_delta.astype(BF16), w_o_cross.astype(BF16), w_out.astype(BF16), _row(ln2_g),
                 w_mlp1.astype(BF16), w_mlp2.astype(BF16), seq=seq, n_mem=n_mem,
                 tm=_pick(seq, 512), tf=1024)


def kernel(x, mem, positions, ln1_g, w_in, w_gate, b_gate, da_qnorm_g, da_knorm_g, da_lambda_q1, da_lambda_k1, da_lambda_q2, da_lambda_k2, da_subln_g, w_o_diff, w_conv, gdn_a_log, gdn_dt_bias, gdn_norm_g, w_o_delta, mem_norm_g, w_mem_kv, ca_qnorm_g, ca_knorm_g, w_o_cross, w_out, ln2_g, w_mlp1, w_mlp2):
    batch, seq, d = x.shape
    n_mem = mem.shape[1]
    x2 = x.reshape(batch * seq, d)
    mem2 = mem.reshape(batch * n_mem, d)
    pos2 = positions.reshape(batch * seq, 1)
    half = DA_QK_DIM // 2
    inv = jnp.exp(-math.log(ROPE_THETA) * jnp.arange(half, dtype=F32) / half)
    inv_row = jnp.tile(inv, LANES // half).reshape(1, LANES)
    params = (ln1_g, w_in, w_gate, b_gate, da_qnorm_g, da_knorm_g, da_lambda_q1, da_lambda_k1,
              da_lambda_q2, da_lambda_k2, da_subln_g, w_o_diff, w_conv, gdn_a_log, gdn_dt_bias,
              gdn_norm_g, w_o_delta, mem_norm_g, w_mem_kv, ca_qnorm_g, ca_knorm_g, w_o_cross, w_out,
              ln2_g, w_mlp1, w_mlp2)
    for l in range(ln1_g.shape[0]):
        x2 = _layer(x2, mem2, pos2, inv_row, batch, seq, n_mem, l, tuple(a[l] for a in params))
    return x2.reshape(batch, seq, d)
```

```python
import functools
import math

import jax
import jax.numpy as jnp
from jax import lax
from jax.experimental import pallas as pl
from jax.experimental.pallas import tpu as pltpu

F32 = jnp.float32
BF16 = jnp.bfloat16

EPS = 1e-6
CHUNK = 64
ROPE_THETA = 10000.0
N_BRANCH = 3

DA_HEADS = 6
DA_QK_DIM = 64
DA_V_DIM = 128
GDN_HEADS = 6
GDN_K_DIM = 128
GDN_V_DIM = 128
CONV_K = 4
CA_HEADS = 4
CA_DIM = 128

LANES = 128
HW = GDN_HEADS * GDN_K_DIM
CA_Q = CA_HEADS * CA_DIM

MXU_N = 256

COL_GQ, COL_GK, COL_GV, COL_GZ, COL_CQ, COL_DV = 0, HW, 2 * HW, 3 * HW, 4 * HW, 4 * HW + CA_Q
N_PROJ = 5 * HW + CA_Q
WCOL_DQ, WCOL_DK, WCOL_AB = N_PROJ, N_PROJ + HW, N_PROJ + 2 * HW
WCOL_GATE = WCOL_AB + MXU_N
AB_B0, AB_A0 = 0, 8

VMEM_LIMIT = 48 * 1024 * 1024
FRONT_VMEM_LIMIT = 56 * 1024 * 1024
BACK_VMEM_LIMIT = 58 * 1024 * 1024


def _cparams(sem):
    return pltpu.CompilerParams(dimension_semantics=sem, vmem_limit_bytes=VMEM_LIMIT)


def _nt_dot(a, b):
    return lax.dot_general(a, b, (((1,), (1,)), ((), ())), preferred_element_type=F32)


def _tn_dot(a, b):
    return lax.dot_general(a, b, (((0,), (0,)), ((), ())), preferred_element_type=F32)


def _dot(a, b):
    return jnp.dot(a, b, preferred_element_type=F32)


def _split_bf16(x):
    hi = x.astype(BF16)
    r = x - hi.astype(F32)
    mid = r.astype(BF16)
    lo = (r - mid.astype(F32)).astype(BF16)
    return hi, mid, lo


def _front_kernel(x_ref, g_ref, wt_ref, wg_ref, bg_ref, pos_ref, inv_ref, gq_ref, gk_ref,
                  proj_ref, ab_ref, gates_ref, qz_ref, kr_ref):
    tm = x_ref.shape[0]
    x = x_ref[...]
    h = (x * lax.rsqrt(jnp.mean(x * x, axis=-1, keepdims=True) + EPS) * g_ref[...]).astype(BF16)

    def cols(c0):
        return _nt_dot(h, wt_ref[c0:c0 + MXU_N, :])

    ang = pos_ref[...].astype(F32) * inv_ref[...]
    cos = jnp.cos(ang)
    sin = jnp.sin(ang)
    lane = lax.broadcasted_iota(jnp.int32, (tm, LANES), 1)
    lo_map = lane < DA_QK_DIM
    first_half = (lane % DA_QK_DIM) < (DA_QK_DIM // 2)
    sin_signed = jnp.where(first_half, -sin, sin)
    scale = DA_QK_DIM ** -0.5 * math.log2(math.e)

    def norm_rope(x, g):
        s = x * x
        tot = jnp.sum(s, axis=-1, keepdims=True)
        lo = jnp.sum(jnp.where(lo_map, s, 0.0), axis=-1, keepdims=True)
        ms = jnp.where(lo_map, lo, tot - lo) * (1.0 / DA_QK_DIM)
        y = x * lax.rsqrt(ms + EPS) * g
        partner = jnp.where(first_half,
                            pltpu.roll(y, LANES - DA_QK_DIM // 2, axis=1),
                            pltpu.roll(y, DA_QK_DIM // 2, axis=1))
        return y * cos + partner * sin_signed

    for c0 in range(0, HW, MXU_N):
        yq = cols(WCOL_DQ + c0)
        yk = cols(WCOL_DK + c0)
        for j in range(MXU_N // LANES):
            sl = slice(c0 + j * LANES, c0 + (j + 1) * LANES)
            q = norm_rope(yq[:, j * LANES:(j + 1) * LANES], gq_ref[...]) * scale
            qz_ref[0, :, sl] = jnp.where(lo_map, q, 0.0).astype(qz_ref.dtype)
            qz_ref[1, :, sl] = jnp.where(lo_map, 0.0, q).astype(qz_ref.dtype)
            k = norm_rope(yk[:, j * LANES:(j + 1) * LANES], gk_ref[...])
            kr_ref[:, sl] = k.astype(kr_ref.dtype)

    ab_ref[...] = cols(WCOL_AB)[:, :LANES]

    for c0 in range(0, gates_ref.shape[1], MXU_N):
        y = _dot(h, wg_ref[:, c0:c0 + MXU_N]) + bg_ref[:, c0:c0 + MXU_N]
        gates_ref[:, c0:c0 + MXU_N] = jax.nn.sigmoid(y).astype(gates_ref.dtype)

    for c0 in range(0, N_PROJ, MXU_N):
        proj_ref[:, c0:c0 + MXU_N] = cols(c0).astype(proj_ref.dtype)


def _front(x, g, w_in_t, w_gate, b_gate, pos, inv, gq, gk, *, tm):
    t, d = x.shape
    n_gate = b_gate.shape[1]

    def tok(n):
        return pl.BlockSpec((tm, n), lambda i: (i, 0))

    def const(shape, **kw):
        return pl.BlockSpec(shape, lambda i: (0,) * len(shape), **kw)

    return pl.pallas_call(
        _front_kernel,
        grid=(t // tm,),
        in_specs=[
            tok(d), const((1, d)),
            const(w_in_t.shape, pipeline_mode=pl.Buffered(1)),
            const(w_gate.shape, pipeline_mode=pl.Buffered(1)),
            const((1, n_gate)), tok(1), const((1, LANES)), const((1, LANES)), const((1, LANES)),
        ],
        out_specs=[
            tok(N_PROJ), tok(LANES), tok(n_gate),
            pl.BlockSpec((2, tm, HW), lambda i: (0, i, 0)),
            tok(HW),
        ],
        out_shape=[
            jax.ShapeDtypeStruct((t, N_PROJ), BF16),
            jax.ShapeDtypeStruct((t, LANES), F32),
            jax.ShapeDtypeStruct((t, n_gate), BF16),
            jax.ShapeDtypeStruct((2, t, HW), BF16),
            jax.ShapeDtypeStruct((t, HW), BF16),
        ],
        compiler_params=pltpu.CompilerParams(dimension_semantics=("parallel",),
                                             vmem_limit_bytes=FRONT_VMEM_LIMIT),
        name="front",
    )(x, g, w_in_t, w_gate, b_gate, pos, inv, gq, gk)


def _diff_attn_kernel(qz_ref, k_ref, v_ref, lam_ref, g_ref, o_ref, m_ref, acc_ref, v1_ref,
                      *, tq, tk, seg, sub, ahead, exp_dtype, lam_init):
    i = pl.program_id(2)
    n_sub = 2 * tq // sub

    @pl.when(i == 0)
    def _():
        v1_ref[:, 0:LANES] = v_ref[...]
        v1_ref[:, LANES:] = jnp.ones((v1_ref.shape[0], LANES), v1_ref.dtype)

    def scores(base, item):
        r, k0, nk = item[:3]
        mp, off = divmod(r * sub, tq)
        return _nt_dot(qz_ref[mp, off:off + sub, :], k_ref[pl.ds(base + k0, nk), :])

    def consume(base, item, s):
        r, k0, nk, masked, first = item
        off = (r * sub) % tq
        rs = slice(r * sub, (r + 1) * sub)
        if masked:
            row = lax.broadcasted_iota(jnp.int32, s.shape, 0) + off
            col = lax.broadcasted_iota(jnp.int32, s.shape, 1) + k0
            s = jnp.where((col // CHUNK) <= (row // CHUNK), s, -jnp.inf)
        m_cur = jnp.max(s, axis=-1, keepdims=True)
        m_new = jnp.broadcast_to(m_cur, (sub, LANES)) if first else jnp.maximum(m_ref[rs], m_cur)
        x = s - jnp.concatenate([m_new] * (nk // LANES), axis=1)
        p = jnp.exp2(x.astype(exp_dtype)).astype(BF16)
        pv = _dot(p, v1_ref[pl.ds(base + k0, nk), :])
        if first:
            acc_ref[rs] = pv
        else:
            alpha = jnp.exp2(m_ref[rs] - m_new)
            acc_ref[rs] = jnp.concatenate([alpha, alpha], axis=1) * acc_ref[rs] + pv
        m_ref[rs] = m_new

    def pipelined(base, items):
        pending = [scores(base, it) for it in items[:ahead]]
        for n, it in enumerate(items):
            s = pending.pop(0)
            if n + ahead < len(items):
                pending.append(scores(base, items[n + ahead]))
            consume(base, it, s)

    diag_items = []
    for kb in range(tq // seg):
        for r in range(n_sub):
            off = (r * sub) % tq
            k0, k1 = kb * seg, min((kb + 1) * seg, off + sub)
            if k1 > k0:
                diag_items.append((r, k0, k1 - k0, k1 > off, kb == 0))
    pipelined(pl.multiple_of(i * tq, tq), diag_items)

    full_items = [(r, kb * seg, seg, False, False) for kb in range(tk // seg) for r in range(n_sub)]

    def body(j, carry):
        pipelined(pl.multiple_of(j * tk, tk), full_items)
        return carry

    lax.fori_loop(0, i * (tq // tk), body, 0)

    o = acc_ref[:, 0:LANES] / acc_ref[:, LANES:]
    lp = lam_ref[...]
    lam = (jnp.exp(jnp.sum(lp[0:1] * lp[1:2], axis=-1, keepdims=True))
           - jnp.exp(jnp.sum(lp[2:3] * lp[3:4], axis=-1, keepdims=True)) + lam_init)
    d = o[:tq] - lam * o[tq:]
    ms = jnp.mean(d * d, axis=-1, keepdims=True)
    o_ref[...] = (d * lax.rsqrt(ms + EPS) * g_ref[...] * (1.0 - lam_init)).astype(o_ref.dtype)


def _diff_attn(qz, kr, proj, lam_pack, g, *, batch, seq, tq, tk, seg, sub, ahead, exp_dtype, lam_init):
    t = batch * seq
    nq = seq // tq
    return pl.pallas_call(
        functools.partial(_diff_attn_kernel, tq=tq, tk=tk, seg=seg, sub=sub, ahead=ahead,
                          exp_dtype=exp_dtype, lam_init=lam_init),
        grid=(batch, DA_HEADS, nq),
        in_specs=[
            pl.BlockSpec((2, tq, LANES), lambda b, h, i: (0, b * nq + i, h)),
            pl.BlockSpec((seq, LANES), lambda b, h, i: (b, h)),
            pl.BlockSpec((seq, LANES), lambda b, h, i: (b, COL_DV // LANES + h)),
            pl.BlockSpec((8, LANES), lambda b, h, i: (0, 0)),
            pl.BlockSpec((1, LANES), lambda b, h, i: (0, 0)),
        ],
        out_specs=pl.BlockSpec((tq, LANES), lambda b, h, i: (b * nq + i, h)),
        out_shape=jax.ShapeDtypeStruct((t, HW), BF16),
        scratch_shapes=[
            pltpu.VMEM((2 * tq, LANES), F32),
            pltpu.VMEM((2 * tq, 2 * LANES), F32),
            pltpu.VMEM((seq, 2 * LANES), BF16),
        ],
        compiler_params=_cparams(("parallel", "parallel", "arbitrary")),
        name="diff_attn",
    )(qz, kr, proj, lam_pack, g)


def _gdn_prep_kernel(q_ref, k_ref, v_ref, qh_ref, kh_ref, vh_ref, ab_ref, wc_ref, alog_ref, dtb_ref,
                     pq_ref, c_ref, cd_ref, op_ref,
                     xp_ref, qn_ref, kn_ref, vn_ref, beta_ref, g_ref, grow_ref, *, ts, cpb):
    first = pl.program_id(1) == 0
    nc = ts // CHUNK

    for s, (cur_ref, halo_ref, dst_ref) in enumerate(
            ((q_ref, qh_ref, qn_ref), (k_ref, kh_ref, kn_ref), (v_ref, vh_ref, vn_ref))):
        halo = halo_ref[...].astype(F32)
        xp_ref[0:8, :] = jnp.where(first, 0.0, halo)
        xp_ref[8:, :] = cur_ref[...].astype(F32)
        acc = xp_ref[8:8 + ts, :] * wc_ref[CONV_K - 1:CONV_K, s * HW:(s + 1) * HW]
        for j in range(1, CONV_K):
            acc = acc + xp_ref[8 - j:8 - j + ts, :] * wc_ref[CONV_K - 1 - j:CONV_K - j, s * HW:(s + 1) * HW]
        act = acc * jax.nn.sigmoid(acc)
        if s < 2:
            for h in range(GDN_HEADS):
                sl = slice(h * LANES, (h + 1) * LANES)
                a_h = act[:, sl]
                dst_ref[:, sl] = a_h * lax.rsqrt(jnp.sum(a_h * a_h, axis=-1, keepdims=True) + EPS)
        else:
            dst_ref[...] = act

    ab = ab_ref[...]
    beta_ref[...] = jax.nn.sigmoid(ab)
    xa = ab + dtb_ref[...]
    softplus = jnp.maximum(xa, 0.0) + jnp.log1p(jnp.exp(-jnp.abs(xa)))
    la = -jnp.exp(alog_ref[...]) * softplus
    ti = lax.broadcasted_iota(jnp.int32, (ts, ts), 0)
    tj = lax.broadcasted_iota(jnp.int32, (ts, ts), 1)
    ltri = jnp.where((tj <= ti) & ((ti // CHUNK) == (tj // CHUNK)), 1.0, 0.0).astype(BF16)
    g_all = sum(_dot(ltri, part) for part in _split_bf16(la))
    g_ref[...] = g_all
    sel_r = lax.broadcasted_iota(jnp.int32, (8, LANES), 0)
    sel_c = lax.broadcasted_iota(jnp.int32, (8, LANES), 1)
    sel = jnp.where(sel_c == sel_r + AB_A0, 1.0, 0.0).astype(BF16)
    grow_all = sum(_nt_dot(sel, part) for part in _split_bf16(g_all))
    for c in range(nc):
        grow_ref[c] = grow_all[:, c * CHUNK:(c + 1) * CHUNK]

    ri = lax.broadcasted_iota(jnp.int32, (CHUNK, CHUNK), 0)
    ci = lax.broadcasted_iota(jnp.int32, (CHUNK, CHUNK), 1)
    incl = ci <= ri
    strict = ci < ri
    scale = GDN_K_DIM ** -0.5

    eye = jnp.where(ri == ci, 1.0, 0.0)
    same_blk = (ri // 16) == (ci // 16)

    def bdot(a, b):
        return _dot(a.astype(BF16), b.astype(BF16))

    def group_body(cg, carry):
        us = []
        for cc in range(cpb):
            c = cg * cpb + cc
            r0 = pl.multiple_of(c * CHUNK, CHUNK)
            rows = pl.ds(r0, CHUNK)
            for h in range(GDN_HEADS):
                sl = slice(h * LANES, (h + 1) * LANES)
                gcol = g_ref[rows, AB_A0 + h:AB_A0 + h + 1]
                glast = g_ref[pl.ds(r0 + CHUNK - 1, 1), AB_A0 + h:AB_A0 + h + 1]
                grow = grow_ref[c, h:h + 1, :]
                us.append(dict(
                    c=c, h=h, rows=rows, sl=sl, gcol=gcol, glast=glast,
                    q=qn_ref[rows, sl] * scale, k=kn_ref[rows, sl], v=vn_ref[rows, sl],
                    b=beta_ref[rows, AB_B0 + h:AB_B0 + h + 1], eg=jnp.exp(gcol),
                    decay=jnp.where(incl, jnp.exp(jnp.where(incl, gcol - grow, 0.0)), 0.0)))

        for u in us:
            kb = u["k"].astype(BF16)
            u["qk_kk"] = _nt_dot(jnp.concatenate([u["q"].astype(BF16), kb], axis=0), kb)
        for u in us:
            a = jnp.where(strict, u["b"] * u["qk_kk"][CHUNK:] * u["decay"], 0.0)
            ad = jnp.where(same_blk, a, 0.0)
            u["ao"] = a - ad
            u["x"] = eye - ad
            u["ad"] = ad
        for u in us:
            u["p"] = bdot(u["ad"], u["ad"])
        for _ in range(2):
            for u in us:
                xp = bdot(u["x"], u["p"])
                u["p"] = bdot(u["p"], u["p"])
                u["x"] = u["x"] + xp
        for u in us:
            u["x"] = u["x"] + bdot(u["x"], u["p"])
        for u in us:
            rhs = jnp.concatenate([u["k"] * (u["b"] * u["eg"]), u["v"] * u["b"]], axis=1)
            u["bm"] = bdot(u["x"], u["ao"])
            u["z"] = bdot(u["x"], rhs)
        for u in us:
            u["b2"] = bdot(u["bm"], u["bm"])
        for u in us:
            u["z"] = u["z"] + bdot(u["b2"], u["z"])
        for u in us:
            u["wu"] = (u["z"] - bdot(u["bm"], u["z"])).astype(BF16)
        for u in us:
            qkm = (u["qk_kk"][:CHUNK] * u["decay"]).astype(BF16)
            kd = (u["k"] * jnp.exp(u["glast"] - u["gcol"])).astype(BF16)
            u["r2"] = _dot(qkm, u["wu"])
            u["r3"] = _tn_dot(kd, u["wu"])
        for u in us:
            c, h = u["c"], u["h"]
            pq_ref[c, h, 0:LANES, :] = u["r3"][:, :LANES].astype(pq_ref.dtype)
            pq_ref[c, h, LANES:, :] = (u["q"] * u["eg"] - u["r2"][:, :LANES]).astype(pq_ref.dtype)
            c_ref[c, h] = u["r3"][:, LANES:].astype(c_ref.dtype)
            cd_ref[c, h] = jnp.broadcast_to(jnp.exp(u["glast"]), (8, LANES))
            op_ref[u["rows"], u["sl"]] = u["r2"][:, LANES:].astype(op_ref.dtype)
        return carry

    lax.fori_loop(0, nc // cpb, group_body, 0)


def _gdn_prep(proj, ab, wconv, alog_row, dtb_row, *, batch, seq, ts, cpb):
    t = batch * seq
    nt = seq // ts
    nc = ts // CHUNK
    n_chunks = t // CHUNK
    hb = ts // 8

    def cur(col):
        return pl.BlockSpec((ts, HW), lambda b, i: (b * nt + i, col // HW))

    def halo(col):
        return pl.BlockSpec((8, HW), lambda b, i: (jnp.maximum((b * nt + i) * hb - 1, 0), col // HW))

    return pl.pallas_call(
        functools.partial(_gdn_prep_kernel, ts=ts, cpb=cpb),
        grid=(batch, nt),
        in_specs=[
            cur(COL_GQ), cur(COL_GK), cur(COL_GV), halo(COL_GQ), halo(COL_GK), halo(COL_GV),
            pl.BlockSpec((ts, LANES), lambda b, i: (b * nt + i, 0)),
            pl.BlockSpec((CONV_K, 3 * HW), lambda b, i: (0, 0)),
            pl.BlockSpec((1, LANES), lambda b, i: (0, 0)),
            pl.BlockSpec((1, LANES), lambda b, i: (0, 0)),
        ],
        out_specs=[
            pl.BlockSpec((nc, GDN_HEADS, GDN_K_DIM + CHUNK, LANES), lambda b, i: (b * nt + i, 0, 0, 0)),
            pl.BlockSpec((nc, GDN_HEADS, GDN_K_DIM, LANES), lambda b, i: (b * nt + i, 0, 0, 0)),
            pl.BlockSpec((nc, GDN_HEADS, 8, LANES), lambda b, i: (b * nt + i, 0, 0, 0)),
            pl.BlockSpec((ts, HW), lambda b, i: (b * nt + i, 0)),
        ],
        out_shape=[
            jax.ShapeDtypeStruct((n_chunks, GDN_HEADS, GDN_K_DIM + CHUNK, LANES), BF16),
            jax.ShapeDtypeStruct((n_chunks, GDN_HEADS, GDN_K_DIM, LANES), BF16),
            jax.ShapeDtypeStruct((n_chunks, GDN_HEADS, 8, LANES), F32),
            jax.ShapeDtypeStruct((t, HW), BF16),
        ],
        scratch_shapes=[
            pltpu.VMEM((ts + 8, HW), F32),
            pltpu.VMEM((ts, HW), F32),
            pltpu.VMEM((ts, HW), F32),
            pltpu.VMEM((ts, HW), F32),
            pltpu.VMEM((ts, LANES), F32),
            pltpu.VMEM((ts, LANES), F32),
            pltpu.VMEM((nc, 8, CHUNK), F32),
        ],
        compiler_params=_cparams(("parallel", "parallel")),
        name="gdn_prep",
    )(proj, proj, proj, proj, proj, proj, ab, wconv, alog_row, dtb_row)


def _gdn_scan_kernel(pq_ref, c_ref, cd_ref, op_ref, z_ref, gn_ref, o_ref, s_ref, *, ts):
    @pl.when(pl.program_id(0) == 0)
    def _():
        s_ref[...] = jnp.zeros(s_ref.shape, F32)

    nc = ts // CHUNK
    batch = s_ref.shape[0]

    def chunk_body(c, carry):
        r0 = pl.multiple_of(c * CHUNK, CHUNK)
        rows = pl.ds(r0, CHUNK)
        for b in range(batch):
            for h in range(GDN_HEADS):
                sl = slice(h * LANES, (h + 1) * LANES)
                s = s_ref[b, h]
                r = _dot(pq_ref[b, c, h], s.astype(BF16))
                o = r[GDN_K_DIM:] + op_ref[b, rows, sl].astype(F32)
                s_ref[b, h] = cd_ref[b, c, h, 0:1, :] * s - r[:GDN_K_DIM] + c_ref[b, c, h].astype(F32)
                on = o * lax.rsqrt(jnp.mean(o * o, axis=-1, keepdims=True) + EPS) * gn_ref[...]
                z = z_ref[b, rows, sl].astype(F32)
                o_ref[b, rows, sl] = (on * (z * jax.nn.sigmoid(z))).astype(o_ref.dtype)
        return carry

    lax.fori_loop(0, nc, chunk_body, 0)


def _gdn_scan(pq, cmat, cd, op, proj, gn, *, batch, seq, ts):
    nt = seq // ts
    nc = ts // CHUNK
    n = seq // CHUNK

    def per_chunk(a):
        return a.reshape((batch, n) + a.shape[1:])

    def chunk_blk(rows):
        return pl.BlockSpec((batch, nc, GDN_HEADS, rows, LANES), lambda i: (0, i, 0, 0, 0))

    def tok_blk(col):
        return pl.BlockSpec((batch, ts, HW), lambda i: (0, i, col))

    out = pl.pallas_call(
        functools.partial(_gdn_scan_kernel, ts=ts),
        grid=(nt,),
        in_specs=[
            chunk_blk(GDN_K_DIM + CHUNK), chunk_blk(GDN_K_DIM), chunk_blk(8),
            tok_blk(0), tok_blk(COL_GZ // HW),
            pl.BlockSpec((1, LANES), lambda i: (0, 0)),
        ],
        out_specs=tok_blk(0),
        out_shape=jax.ShapeDtypeStruct((batch, seq, HW), BF16),
        scratch_shapes=[pltpu.VMEM((batch, GDN_HEADS, GDN_K_DIM, GDN_V_DIM), F32)],
        compiler_params=_cparams(("arbitrary",)),
        name="gdn_scan",
    )(per_chunk(pq), per_chunk(cmat), per_chunk(cd), op.reshape(batch, seq, HW),
      proj.reshape(batch, seq, proj.shape[1]), gn)
    return out.reshape(batch * seq, HW)


def _mem_kv_kernel(m_ref, g_ref, w_ref, gk_ref, k_ref, v_ref):
    m = m_ref[...]
    mn = m * lax.rsqrt(jnp.mean(m * m, axis=-1, keepdims=True) + EPS) * g_ref[...]
    kv = _dot(mn.astype(BF16), w_ref[...])
    for h in range(CA_HEADS):
        sl = slice(h * LANES, (h + 1) * LANES)
        kh = kv[:, sl]
        k_ref[:, sl] = (kh * lax.rsqrt(jnp.mean(kh * kh, axis=-1, keepdims=True) + EPS)
                        * gk_ref[...]).astype(k_ref.dtype)
    v_ref[...] = kv[:, CA_Q:].astype(v_ref.dtype)


def _mem_kv(mem2, g, w, gk, *, batch, n_mem):
    d = mem2.shape[1]
    return pl.pallas_call(
        _mem_kv_kernel,
        grid=(batch,),
        in_specs=[
            pl.BlockSpec((n_mem, d), lambda b: (b, 0)),
            pl.BlockSpec((1, d), lambda b: (0, 0)),
            pl.BlockSpec((d, 2 * CA_Q), lambda b: (0, 0)),
            pl.BlockSpec((1, LANES), lambda b: (0, 0)),
        ],
        out_specs=[
            pl.BlockSpec((n_mem, CA_Q), lambda b: (b, 0)),
            pl.BlockSpec((n_mem, CA_Q), lambda b: (b, 0)),
        ],
        out_shape=[
            jax.ShapeDtypeStruct((batch * n_mem, CA_Q), BF16),
            jax.ShapeDtypeStruct((batch * n_mem, CA_Q), BF16),
        ],
        compiler_params=_cparams(("parallel",)),
        name="mem_kv",
    )(mem2, g, w, gk)


def _back_kernel(x_ref, oa_ref, ob_ref, cq_ref, gt_ref, kc_ref, vc_ref, gcq_ref,
                 wa_ref, wb_ref, wc_ref, wo_ref, g2_ref, w1_ref, w2_ref, o_ref, *, tf):
    d = x_ref.shape[1]
    heads = []
    for h in range(CA_HEADS):
        sl = slice(h * LANES, (h + 1) * LANES)
        q = cq_ref[:, sl].astype(F32)
        qn = (q * lax.rsqrt(jnp.mean(q * q, axis=-1, keepdims=True) + EPS) * gcq_ref[...]
              * (CA_DIM ** -0.5))
        s = _nt_dot(qn.astype(BF16), kc_ref[:, sl])
        p = jnp.exp(s - jnp.max(s, axis=-1, keepdims=True))
        l = jnp.sum(p, axis=-1, keepdims=True)
        heads.append((_dot(p.astype(BF16), vc_ref[:, sl]) / l).astype(BF16))
    oc = jnp.concatenate(heads, axis=1)
    ya = _dot(oa_ref[...], wa_ref[...])
    yb = _dot(ob_ref[...], wb_ref[...])
    yc = _dot(oc, wc_ref[...])
    mixed = (gt_ref[:, 0:d].astype(F32) * ya + gt_ref[:, d:2 * d].astype(F32) * yb
             + gt_ref[:, 2 * d:3 * d].astype(F32) * yc)
    x1 = x_ref[...] + _dot(mixed.astype(BF16), wo_ref[...])
    ms = jnp.mean(x1 * x1, axis=-1, keepdims=True)
    h2 = (x1 * lax.rsqrt(ms + EPS) * g2_ref[...]).astype(BF16)
    o_ref[...] = x1
    for c0 in range(0, w1_ref.shape[1], tf):
        a = jnp.maximum(_dot(h2, w1_ref[:, c0:c0 + tf]), 0.0)
        o_ref[...] += _dot((a * a).astype(BF16), w2_ref[c0:c0 + tf, :])


def _back(x, oa, ob, proj, gates, kc, vc, gcq, wa, wb, wc, wo, g2, w1, w2, *, seq, n_mem, tm, tf):
    t, d = x.shape
    tiles_per_seq = seq // tm

    def tok(n, col=0):
        return pl.BlockSpec((tm, n), lambda i: (i, col))

    def const(a):
        return pl.BlockSpec(a.shape, lambda i: (0, 0), pipeline_mode=pl.Buffered(1))

    def mem_blk():
        return pl.BlockSpec((n_mem, CA_Q), lambda i: (i // tiles_per_seq, 0))

    return pl.pallas_call(
        functools.partial(_back_kernel, tf=tf),
        grid=(t // tm,),
        in_specs=[
            tok(d), tok(HW), tok(HW), tok(CA_Q, COL_CQ // CA_Q), tok(N_BRANCH * d),
            mem_blk(), mem_blk(), const(gcq), const(wa), const(wb), const(wc), const(wo), const(g2),
            const(w1), const(w2),
        ],
        out_specs=tok(d),
        out_shape=jax.ShapeDtypeStruct((t, d), F32),
        compiler_params=pltpu.CompilerParams(dimension_semantics=("parallel",),
                                             vmem_limit_bytes=BACK_VMEM_LIMIT),
        name="back",
    )(x, oa, ob, proj, gates, kc, vc, gcq, wa, wb, wc, wo, g2, w1, w2)


def _pick(n, pref):
    t = min(pref, n)
    while n % t:
        t //= 2
    return t


def _row(v, width=None):
    v = v.astype(F32).reshape(1, -1)
    if width is not None and v.shape[1] < width:
        v = jnp.pad(v, ((0, 0), (0, width - v.shape[1])))
    return v


def _layer(x2, mem2, pos2, inv_row, batch, seq, n_mem, l, p):
    t, d = x2.shape
    (ln1_g, w_in, w_gate, b_gate, da_qnorm_g, da_knorm_g, lq1, lk1, lq2, lk2, da_subln_g, w_o_diff,
     w_conv, gdn_a_log, gdn_dt_bias, gdn_norm_g, w_o_delta, mem_norm_g, w_mem_kv, ca_qnorm_g,
     ca_knorm_g, w_o_cross, w_out, ln2_g, w_mlp1, w_mlp2) = p

    o = [0]
    for n in (HW, HW, HW, HW, HW, HW, HW, GDN_HEADS, GDN_HEADS, CA_Q):
        o.append(o[-1] + n)
    w_t = w_in.T
    sec = [w_t[o[i]:o[i + 1]] for i in range(10)]
    w_ab = jnp.zeros((MXU_N, d), F32)
    w_ab = w_ab.at[AB_B0:AB_B0 + GDN_HEADS].set(sec[7]).at[AB_A0:AB_A0 + GDN_HEADS].set(sec[8])
    w_in_t = jnp.concatenate([sec[3], sec[4], sec[5], sec[6], sec[9], sec[2], sec[0], sec[1], w_ab],
                             axis=0).astype(BF16)
    gq = _row(jnp.tile(da_qnorm_g, 2))
    gk = _row(jnp.tile(da_knorm_g, 2))
    proj, ab, gates, qz, kr = _front(x2, _row(ln1_g), w_in_t, w_gate.astype(BF16), _row(b_gate), pos2,
                                     inv_row, gq, gk, tm=_pick(t, 512))

    lam_pack = jnp.zeros((8, LANES), F32)
    lam_pack = lam_pack.at[0:4, 0:DA_QK_DIM].set(jnp.stack([lq1, lk1, lq2, lk2]).astype(F32))
    lam_init = 0.8 - 0.6 * math.exp(-0.3 * l)
    oa = _diff_attn(qz, kr, proj, lam_pack, _row(da_subln_g), batch=batch, seq=seq,
                    tq=_pick(seq, 1024), tk=_pick(seq, 1024), seg=_pick(seq, 512), sub=256, ahead=2,
                    exp_dtype=BF16, lam_init=lam_init)

    alog_row = jnp.zeros((1, LANES), F32).at[0, AB_A0:AB_A0 + GDN_HEADS].set(gdn_a_log.astype(F32))
    dtb_row = jnp.zeros((1, LANES), F32).at[0, AB_A0:AB_A0 + GDN_HEADS].set(gdn_dt_bias.astype(F32))
    pq, cmat, cd, op = _gdn_prep(proj, ab, w_conv.astype(F32), alog_row, dtb_row,
                                 batch=batch, seq=seq, ts=_pick(seq, 256), cpb=4)
    ob = _gdn_scan(pq, cmat, cd, op, proj, _row(gdn_norm_g), batch=batch, seq=seq, ts=_pick(seq, 256))

    kc, vc = _mem_kv(mem2, _row(mem_norm_g), w_mem_kv.astype(BF16), _row(ca_knorm_g),
                     batch=batch, n_mem=n_mem)

    return _back(x2, oa, ob, proj, gates, kc, vc, _row(ca_qnorm_g), w_o_diff.astype(BF16),
                 w_o_delta.astype(BF16), w_o_cross.astype(BF16), w_out.astype(BF16), _row(ln2_g),
                 w_mlp1.astype(BF16), w_mlp2.astype(BF16), seq=seq, n_mem=n_mem,
                 tm=_pick(seq, 512), tf=1024)


def kernel(x, mem, positions, ln1_g, w_in, w_gate, b_gate, da_qnorm_g, da_knorm_g, da_lambda_q1, da_lambda_k1, da_lambda_q2, da_lambda_k2, da_subln_g, w_o_diff, w_conv, gdn_a_log, gdn_dt_bias, gdn_norm_g, w_o_delta, mem_norm_g, w_mem_kv, ca_qnorm_g, ca_knorm_g, w_o_cross, w_out, ln2_g, w_mlp1, w_mlp2):
    batch, seq, d = x.shape
    n_mem = mem.shape[1]
    x2 = x.reshape(batch * seq, d)
    mem2 = mem.reshape(batch * n_mem, d)
    pos2 = positions.reshape(batch * seq, 1)
    half = DA_QK_DIM // 2
    inv = jnp.exp(-math.log(ROPE_THETA) * jnp.arange(half, dtype=F32) / half)
    inv_row = jnp.tile(inv, LANES // half).reshape(1, LANES)
    params = (ln1_g, w_in, w_gate, b_gate, da_qnorm_g, da_knorm_g, da_lambda_q1, da_lambda_k1,
              da_lambda_q2, da_lambda_k2, da_subln_g, w_o_diff, w_conv, gdn_a_log, gdn_dt_bias,
              gdn_norm_g, w_o_delta, mem_norm_g, w_mem_kv, ca_qnorm_g, ca_knorm_g, w_o_cross, w_out,
              ln2_g, w_mlp1, w_mlp2)
    for l in range(ln1_g.shape[0]):
        x2 = _layer(x2, mem2, pos2, inv_row, batch, seq, n_mem, l, tuple(a[l] for a in params))
    return x2.reshape(batch, seq, d)
```

```python
import functools
import math

import jax
import jax.numpy as jnp
from jax import lax
from jax.experimental import pallas as pl
from jax.experimental.pallas import tpu as pltpu

F32 = jnp.float32
BF16 = jnp.bfloat16

EPS = 1e-6
CHUNK = 64
ROPE_THETA = 10000.0
N_BRANCH = 3

DA_HEADS = 6
DA_QK_DIM = 64
DA_V_DIM = 128
GDN_HEADS = 6
GDN_K_DIM = 128
GDN_V_DIM = 128
CONV_K = 4
CA_HEADS = 4
CA_DIM = 128

LANES = 128
HW = GDN_HEADS * GDN_K_DIM
CA_Q = CA_HEADS * CA_DIM

MXU_N = 256

COL_GQ, COL_GK, COL_GV, COL_GZ, COL_CQ, COL_DV = 0, HW, 2 * HW, 3 * HW, 4 * HW, 4 * HW + CA_Q
N_PROJ = 5 * HW + CA_Q
WCOL_DQ, WCOL_DK, WCOL_AB = N_PROJ, N_PROJ + HW, N_PROJ + 2 * HW
WCOL_GATE = WCOL_AB + MXU_N
AB_B0, AB_A0 = 0, 8

VMEM_LIMIT = 48 * 1024 * 1024
FRONT_VMEM_LIMIT = 56 * 1024 * 1024
BACK_VMEM_LIMIT = 58 * 1024 * 1024


def _cparams(sem):
    return pltpu.CompilerParams(dimension_semantics=sem, vmem_limit_bytes=VMEM_LIMIT)


def _nt_dot(a, b):
    return lax.dot_general(a, b, (((1,), (1,)), ((), ())), preferred_element_type=F32)


def _tn_dot(a, b):
    return lax.dot_general(a, b, (((0,), (0,)), ((), ())), preferred_element_type=F32)


def _dot(a, b):
    return jnp.dot(a, b, preferred_element_type=F32)


def _split_bf16(x):
    hi = x.astype(BF16)
    r = x - hi.astype(F32)
    mid = r.astype(BF16)
    lo = (r - mid.astype(F32)).astype(BF16)
    return hi, mid, lo


def _front_kernel(x_ref, g_ref, wt_ref, wg_ref, bg_ref, pos_ref, inv_ref, gq_ref, gk_ref,
                  proj_ref, ab_ref, gates_ref, qz_ref, kr_ref):
    tm = x_ref.shape[0]
    x = x_ref[...]
    h = (x * lax.rsqrt(jnp.mean(x * x, axis=-1, keepdims=True) + EPS) * g_ref[...]).astype(BF16)

    def cols(c0):
        return _nt_dot(h, wt_ref[c0:c0 + MXU_N, :])

    ang = pos_ref[...].astype(F32) * inv_ref[...]
    cos = jnp.cos(ang)
    sin = jnp.sin(ang)
    lane = lax.broadcasted_iota(jnp.int32, (tm, LANES), 1)
    lo_map = lane < DA_QK_DIM
    first_half = (lane % DA_QK_DIM) < (DA_QK_DIM // 2)
    sin_signed = jnp.where(first_half, -sin, sin)
    scale = DA_QK_DIM ** -0.5 * math.log2(math.e)

    def norm_rope(x, g):
        s = x * x
        tot = jnp.sum(s, axis=-1, keepdims=True)
        lo = jnp.sum(jnp.where(lo_map, s, 0.0), axis=-1, keepdims=True)
        ms = jnp.where(lo_map, lo, tot - lo) * (1.0 / DA_QK_DIM)
        y = x * lax.rsqrt(ms + EPS) * g
        partner = jnp.where(first_half,
                            pltpu.roll(y, LANES - DA_QK_DIM // 2, axis=1),
                            pltpu.roll(y, DA_QK_DIM // 2, axis=1))
        return y * cos + partner * sin_signed

    for c0 in range(0, HW, MXU_N):
        yq = cols(WCOL_DQ + c0)
        yk = cols(WCOL_DK + c0)
        for j in range(MXU_N // LANES):
            sl = slice(c0 + j * LANES, c0 + (j + 1) * LANES)
            q = norm_rope(yq[:, j * LANES:(j + 1) * LANES], gq_ref[...]) * scale
            qz_ref[0, :, sl] = jnp.where(lo_map, q, 0.0).astype(qz_ref.dtype)
            qz_ref[1, :, sl] = jnp.where(lo_map, 0.0, q).astype(qz_ref.dtype)
            k = norm_rope(yk[:, j * LANES:(j + 1) * LANES], gk_ref[...])
            kr_ref[:, sl] = k.astype(kr_ref.dtype)

    ab_ref[...] = cols(WCOL_AB)[:, :LANES]

    for c0 in range(0, gates_ref.shape[1], MXU_N):
        y = _dot(h, wg_ref[:, c0:c0 + MXU_N]) + bg_ref[:, c0:c0 + MXU_N]
        gates_ref[:, c0:c0 + MXU_N] = jax.nn.sigmoid(y).astype(gates_ref.dtype)

    for c0 in range(0, N_PROJ, MXU_N):
        proj_ref[:, c0:c0 + MXU_N] = cols(c0).astype(proj_ref.dtype)


def _front(x, g, w_in_t, w_gate, b_gate, pos, inv, gq, gk, *, tm):
    t, d = x.shape
    n_gate = b_gate.shape[1]

    def tok(n):
        return pl.BlockSpec((tm, n), lambda i: (i, 0))

    def const(shape, **kw):
        return pl.BlockSpec(shape, lambda i: (0,) * len(shape), **kw)

    return pl.pallas_call(
        _front_kernel,
        grid=(t // tm,),
        in_specs=[
            tok(d), const((1, d)),
            const(w_in_t.shape, pipeline_mode=pl.Buffered(1)),
            const(w_gate.shape, pipeline_mode=pl.Buffered(1)),
            const((1, n_gate)), tok(1), const((1, LANES)), const((1, LANES)), const((1, LANES)),
        ],
        out_specs=[
            tok(N_PROJ), tok(LANES), tok(n_gate),
            pl.BlockSpec((2, tm, HW), lambda i: (0, i, 0)),
            tok(HW),
        ],
        out_shape=[
            jax.ShapeDtypeStruct((t, N_PROJ), BF16),
            jax.ShapeDtypeStruct((t, LANES), F32),
            jax.ShapeDtypeStruct((t, n_gate), BF16),
            jax.ShapeDtypeStruct((2, t, HW), BF16),
            jax.ShapeDtypeStruct((t, HW), BF16),
        ],
        compiler_params=pltpu.CompilerParams(dimension_semantics=("parallel",),
                                             vmem_limit_bytes=FRONT_VMEM_LIMIT),
        name="front",
    )(x, g, w_in_t, w_gate, b_gate, pos, inv, gq, gk)


def _diff_attn_kernel(qz_ref, k_ref, v_ref, lam_ref, g_ref, o_ref, m_ref, acc_ref, v1_ref,
                      *, tq, tk, seg, sub, ahead, unroll, exp_dtype, lam_init):
    i = pl.program_id(2)
    n_sub = 2 * tq // sub

    @pl.when(i == 0)
    def _():
        v1_ref[:, 0:LANES] = v_ref[...]
        v1_ref[:, LANES:] = jnp.ones((v1_ref.shape[0], LANES), v1_ref.dtype)

    def scores(base, item):
        r, k0, nk = item[:3]
        mp, off = divmod(r * sub, tq)
        return _nt_dot(qz_ref[mp, off:off + sub, :], k_ref[pl.ds(base + k0, nk), :])

    def consume(base, item, s):
        r, k0, nk, masked, first = item
        off = (r * sub) % tq
        rs = slice(r * sub, (r + 1) * sub)
        if masked:
            row = lax.broadcasted_iota(jnp.int32, s.shape, 0) + off
            col = lax.broadcasted_iota(jnp.int32, s.shape, 1) + k0
            s = jnp.where((col // CHUNK) <= (row // CHUNK), s, -jnp.inf)
        m_cur = jnp.max(s, axis=-1, keepdims=True)
        m_new = jnp.broadcast_to(m_cur, (sub, LANES)) if first else jnp.maximum(m_ref[rs], m_cur)
        x = s - jnp.concatenate([m_new] * (nk // LANES), axis=1)
        p = jnp.exp2(x.astype(exp_dtype)).astype(BF16)
        pv = _dot(p, v1_ref[pl.ds(base + k0, nk), :])
        if first:
            acc_ref[rs] = pv
        else:
            alpha = jnp.exp2(m_ref[rs] - m_new)
            acc_ref[rs] = jnp.concatenate([alpha, alpha], axis=1) * acc_ref[rs] + pv
        m_ref[rs] = m_new

    def pipelined(base, items):
        pending = [scores(base, it) for it in items[:ahead]]
        for n, it in enumerate(items):
            s = pending.pop(0)
            if n + ahead < len(items):
                pending.append(scores(base, items[n + ahead]))
            consume(base, it, s)

    diag_items = []
    for kb in range(tq // seg):
        for r in range(n_sub):
            off = (r * sub) % tq
            k0, k1 = kb * seg, min((kb + 1) * seg, off + sub)
            if k1 > k0:
                diag_items.append((r, k0, k1 - k0, k1 > off, kb == 0))
    pipelined(pl.multiple_of(i * tq, tq), diag_items)

    def full_items(n_blocks):
        return [(r, kb * seg, seg, False, False)
                for kb in range(n_blocks * tk // seg) for r in range(n_sub)]

    n_full = i * (tq // tk)

    def body(j, carry):
        pipelined(pl.multiple_of(j * (unroll * tk), unroll * tk), full_items(unroll))
        return carry

    lax.fori_loop(0, n_full // unroll, body, 0)

    group = unroll // 2
    while group >= 1:
        done = (n_full // (2 * group)) * (2 * group)

        @pl.when(n_full % (2 * group) >= group)
        def _(group=group, done=done):
            pipelined(pl.multiple_of(done * tk, tk), full_items(group))

        group //= 2

    o = acc_ref[:, 0:LANES] / acc_ref[:, LANES:]
    lp = lam_ref[...]
    lam = (jnp.exp(jnp.sum(lp[0:1] * lp[1:2], axis=-1, keepdims=True))
           - jnp.exp(jnp.sum(lp[2:3] * lp[3:4], axis=-1, keepdims=True)) + lam_init)
    d = o[:tq] - lam * o[tq:]
    ms = jnp.mean(d * d, axis=-1, keepdims=True)
    o_ref[...] = (d * lax.rsqrt(ms + EPS) * g_ref[...] * (1.0 - lam_init)).astype(o_ref.dtype)


def _diff_attn(qz, kr, proj, lam_pack, g, *, batch, seq, tq, tk, seg, sub, ahead, unroll, exp_dtype,
               lam_init):
    t = batch * seq
    nq = seq // tq
    return pl.pallas_call(
        functools.partial(_diff_attn_kernel, tq=tq, tk=tk, seg=seg, sub=sub, ahead=ahead, unroll=unroll,
                          exp_dtype=exp_dtype, lam_init=lam_init),
        grid=(batch, DA_HEADS, nq),
        in_specs=[
            pl.BlockSpec((2, tq, LANES), lambda b, h, i: (0, b * nq + i, h)),
            pl.BlockSpec((seq, LANES), lambda b, h, i: (b, h)),
            pl.BlockSpec((seq, LANES), lambda b, h, i: (b, COL_DV // LANES + h)),
            pl.BlockSpec((8, LANES), lambda b, h, i: (0, 0)),
            pl.BlockSpec((1, LANES), lambda b, h, i: (0, 0)),
        ],
        out_specs=pl.BlockSpec((tq, LANES), lambda b, h, i: (b * nq + i, h)),
        out_shape=jax.ShapeDtypeStruct((t, HW), BF16),
        scratch_shapes=[
            pltpu.VMEM((2 * tq, LANES), F32),
            pltpu.VMEM((2 * tq, 2 * LANES), F32),
            pltpu.VMEM((seq, 2 * LANES), BF16),
        ],
        compiler_params=_cparams(("parallel", "parallel", "arbitrary")),
        name="diff_attn",
    )(qz, kr, proj, lam_pack, g)


def _gdn_prep_kernel(q_ref, k_ref, v_ref, qh_ref, kh_ref, vh_ref, ab_ref, wc_ref, alog_ref, dtb_ref,
                     pq_ref, c_ref, cd_ref, op_ref,
                     xp_ref, qn_ref, kn_ref, vn_ref, beta_ref, g_ref, grow_ref, *, ts, cpb):
    first = pl.program_id(1) == 0
    nc = ts // CHUNK

    for s, (cur_ref, halo_ref, dst_ref) in enumerate(
            ((q_ref, qh_ref, qn_ref), (k_ref, kh_ref, kn_ref), (v_ref, vh_ref, vn_ref))):
        halo = halo_ref[...].astype(F32)
        xp_ref[0:8, :] = jnp.where(first, 0.0, halo)
        xp_ref[8:, :] = cur_ref[...].astype(F32)
        acc = xp_ref[8:8 + ts, :] * wc_ref[CONV_K - 1:CONV_K, s * HW:(s + 1) * HW]
        for j in range(1, CONV_K):
            acc = acc + xp_ref[8 - j:8 - j + ts, :] * wc_ref[CONV_K - 1 - j:CONV_K - j, s * HW:(s + 1) * HW]
        half = 0.5 * acc
        act = half + half * jnp.tanh(half)
        if s < 2:
            for h in range(GDN_HEADS):
                sl = slice(h * LANES, (h + 1) * LANES)
                a_h = act[:, sl]
                dst_ref[:, sl] = a_h * lax.rsqrt(jnp.sum(a_h * a_h, axis=-1, keepdims=True) + EPS)
        else:
            dst_ref[...] = act

    ab = ab_ref[...]
    beta_ref[...] = jax.nn.sigmoid(ab)
    xa = ab + dtb_ref[...]
    softplus = jnp.maximum(xa, 0.0) + jnp.log1p(jnp.exp(-jnp.abs(xa)))
    la = -jnp.exp(alog_ref[...]) * softplus
    ti = lax.broadcasted_iota(jnp.int32, (ts, ts), 0)
    tj = lax.broadcasted_iota(jnp.int32, (ts, ts), 1)
    ltri = jnp.where((tj <= ti) & ((ti // CHUNK) == (tj // CHUNK)), 1.0, 0.0).astype(BF16)
    g_all = sum(_dot(ltri, part) for part in _split_bf16(la))
    g_ref[...] = g_all
    sel_r = lax.broadcasted_iota(jnp.int32, (8, LANES), 0)
    sel_c = lax.broadcasted_iota(jnp.int32, (8, LANES), 1)
    sel = jnp.where(sel_c == sel_r + AB_A0, 1.0, 0.0).astype(BF16)
    grow_all = sum(_nt_dot(sel, part) for part in _split_bf16(g_all))
    for c in range(nc):
        grow_ref[c] = grow_all[:, c * CHUNK:(c + 1) * CHUNK]

    ri = lax.broadcasted_iota(jnp.int32, (CHUNK, CHUNK), 0)
    ci = lax.broadcasted_iota(jnp.int32, (CHUNK, CHUNK), 1)
    incl = ci <= ri
    strict = ci < ri
    scale = GDN_K_DIM ** -0.5

    eye = jnp.where(ri == ci, 1.0, 0.0)
    same_blk = (ri // 16) == (ci // 16)

    def bdot(a, b):
        return _dot(a.astype(BF16), b.astype(BF16))

    def group_body(cg, carry):
        us = []
        for cc in range(cpb):
            c = cg * cpb + cc
            r0 = pl.multiple_of(c * CHUNK, CHUNK)
            rows = pl.ds(r0, CHUNK)
            for h in range(GDN_HEADS):
                sl = slice(h * LANES, (h + 1) * LANES)
                gcol = g_ref[rows, AB_A0 + h:AB_A0 + h + 1]
                glast = g_ref[pl.ds(r0 + CHUNK - 1, 1), AB_A0 + h:AB_A0 + h + 1]
                grow = grow_ref[c, h:h + 1, :]
                us.append(dict(
                    c=c, h=h, rows=rows, sl=sl, gcol=gcol, glast=glast,
                    q=qn_ref[rows, sl] * scale, k=kn_ref[rows, sl], v=vn_ref[rows, sl],
                    b=beta_ref[rows, AB_B0 + h:AB_B0 + h + 1], eg=jnp.exp(gcol),
                    decay=jnp.where(incl, jnp.exp(jnp.where(incl, gcol - grow, 0.0)), 0.0)))

        for u in us:
            kb = u["k"].astype(BF16)
            u["qk_kk"] = _nt_dot(jnp.concatenate([u["q"].astype(BF16), kb], axis=0), kb)
        for u in us:
            a = jnp.where(strict, u["b"] * u["qk_kk"][CHUNK:] * u["decay"], 0.0)
            ad = jnp.where(same_blk, a, 0.0)
            u["ao"] = a - ad
            u["x"] = eye - ad
            u["ad"] = ad
        for u in us:
            u["p"] = bdot(u["ad"], u["ad"])
        for _ in range(2):
            for u in us:
                xp = bdot(u["x"], u["p"])
                u["p"] = bdot(u["p"], u["p"])
                u["x"] = u["x"] + xp
        for u in us:
            u["x"] = u["x"] + bdot(u["x"], u["p"])
        for u in us:
            rhs = jnp.concatenate([u["k"] * (u["b"] * u["eg"]), u["v"] * u["b"]], axis=1)
            u["bm"] = bdot(u["x"], u["ao"])
            u["z"] = bdot(u["x"], rhs)
        for u in us:
            u["b2"] = bdot(u["bm"], u["bm"])
        for u in us:
            u["z"] = u["z"] + bdot(u["b2"], u["z"])
        for u in us:
            u["wu"] = (u["z"] - bdot(u["bm"], u["z"])).astype(BF16)
        for u in us:
            qkm = (u["qk_kk"][:CHUNK] * u["decay"]).astype(BF16)
            kd = (u["k"] * jnp.exp(u["glast"] - u["gcol"])).astype(BF16)
            u["r2"] = _dot(qkm, u["wu"])
            u["r3"] = _tn_dot(kd, u["wu"])
        for u in us:
            c, h = u["c"], u["h"]
            pq_ref[c, h, 0:LANES, :] = u["r3"][:, :LANES].astype(pq_ref.dtype)
            pq_ref[c, h, LANES:, :] = (u["q"] * u["eg"] - u["r2"][:, :LANES]).astype(pq_ref.dtype)
            c_ref[c, h] = u["r3"][:, LANES:].astype(c_ref.dtype)
            cd_ref[c, h] = jnp.broadcast_to(jnp.exp(u["glast"]), (8, LANES))
            op_ref[u["rows"], u["sl"]] = u["r2"][:, LANES:].astype(op_ref.dtype)
        return carry

    lax.fori_loop(0, nc // cpb, group_body, 0)


def _gdn_prep(proj, ab, wconv, alog_row, dtb_row, *, batch, seq, ts, cpb):
    t = batch * seq
    nt = seq // ts
    nc = ts // CHUNK
    n_chunks = t // CHUNK
    hb = ts // 8

    def cur(col):
        return pl.BlockSpec((ts, HW), lambda b, i: (b * nt + i, col // HW))

    def halo(col):
        return pl.BlockSpec((8, HW), lambda b, i: (jnp.maximum((b * nt + i) * hb - 1, 0), col // HW))

    return pl.pallas_call(
        functools.partial(_gdn_prep_kernel, ts=ts, cpb=cpb),
        grid=(batch, nt),
        in_specs=[
            cur(COL_GQ), cur(COL_GK), cur(COL_GV), halo(COL_GQ), halo(COL_GK), halo(COL_GV),
            pl.BlockSpec((ts, LANES), lambda b, i: (b * nt + i, 0)),
            pl.BlockSpec((CONV_K, 3 * HW), lambda b, i: (0, 0)),
            pl.BlockSpec((1, LANES), lambda b, i: (0, 0)),
            pl.BlockSpec((1, LANES), lambda b, i: (0, 0)),
        ],
        out_specs=[
            pl.BlockSpec((nc, GDN_HEADS, GDN_K_DIM + CHUNK, LANES), lambda b, i: (b * nt + i, 0, 0, 0)),
            pl.BlockSpec((nc, GDN_HEADS, GDN_K_DIM, LANES), lambda b, i: (b * nt + i, 0, 0, 0)),
            pl.BlockSpec((nc, GDN_HEADS, 8, LANES), lambda b, i: (b * nt + i, 0, 0, 0)),
            pl.BlockSpec((ts, HW), lambda b, i: (b * nt + i, 0)),
        ],
        out_shape=[
            jax.ShapeDtypeStruct((n_chunks, GDN_HEADS, GDN_K_DIM + CHUNK, LANES), BF16),
            jax.ShapeDtypeStruct((n_chunks, GDN_HEADS, GDN_K_DIM, LANES), BF16),
            jax.ShapeDtypeStruct((n_chunks, GDN_HEADS, 8, LANES), F32),
            jax.ShapeDtypeStruct((t, HW), BF16),
        ],
        scratch_shapes=[
            pltpu.VMEM((ts + 8, HW), F32),
            pltpu.VMEM((ts, HW), F32),
            pltpu.VMEM((ts, HW), F32),
            pltpu.VMEM((ts, HW), F32),
            pltpu.VMEM((ts, LANES), F32),
            pltpu.VMEM((ts, LANES), F32),
            pltpu.VMEM((nc, 8, CHUNK), F32),
        ],
        compiler_params=_cparams(("parallel", "parallel")),
        name="gdn_prep",
    )(proj, proj, proj, proj, proj, proj, ab, wconv, alog_row, dtb_row)


def _gdn_scan_kernel(pq_ref, c_ref, cd_ref, op_ref, z_ref, gn_ref, o_ref, s_ref, *, ts):
    @pl.when(pl.program_id(0) == 0)
    def _():
        s_ref[...] = jnp.zeros(s_ref.shape, F32)

    nc = ts // CHUNK
    batch = s_ref.shape[0]

    def chunk_body(c, carry):
        r0 = pl.multiple_of(c * CHUNK, CHUNK)
        rows = pl.ds(r0, CHUNK)
        for b in range(batch):
            for h in range(GDN_HEADS):
                sl = slice(h * LANES, (h + 1) * LANES)
                s = s_ref[b, h]
                r = _dot(pq_ref[b, c, h], s.astype(BF16))
                o = r[GDN_K_DIM:] + op_ref[b, rows, sl].astype(F32)
                s_ref[b, h] = cd_ref[b, c, h, 0:1, :] * s - r[:GDN_K_DIM] + c_ref[b, c, h].astype(F32)
                on = o * lax.rsqrt(jnp.mean(o * o, axis=-1, keepdims=True) + EPS) * gn_ref[...]
                z = z_ref[b, rows, sl].astype(F32)
                o_ref[b, rows, sl] = (on * (z * jax.nn.sigmoid(z))).astype(o_ref.dtype)
        return carry

    lax.fori_loop(0, nc, chunk_body, 0)


def _gdn_scan(pq, cmat, cd, op, proj, gn, *, batch, seq, ts):
    nt = seq // ts
    nc = ts // CHUNK
    n = seq // CHUNK

    def per_chunk(a):
        return a.reshape((batch, n) + a.shape[1:])

    def chunk_blk(rows):
        return pl.BlockSpec((batch, nc, GDN_HEADS, rows, LANES), lambda i: (0, i, 0, 0, 0))

    def tok_blk(col):
        return pl.BlockSpec((batch, ts, HW), lambda i: (0, i, col))

    out = pl.pallas_call(
        functools.partial(_gdn_scan_kernel, ts=ts),
        grid=(nt,),
        in_specs=[
            chunk_blk(GDN_K_DIM + CHUNK), chunk_blk(GDN_K_DIM), chunk_blk(8),
            tok_blk(0), tok_blk(COL_GZ // HW),
            pl.BlockSpec((1, LANES), lambda i: (0, 0)),
        ],
        out_specs=tok_blk(0),
        out_shape=jax.ShapeDtypeStruct((batch, seq, HW), BF16),
        scratch_shapes=[pltpu.VMEM((batch, GDN_HEADS, GDN_K_DIM, GDN_V_DIM), F32)],
        compiler_params=_cparams(("arbitrary",)),
        name="gdn_scan",
    )(per_chunk(pq), per_chunk(cmat), per_chunk(cd), op.reshape(batch, seq, HW),
      proj.reshape(batch, seq, proj.shape[1]), gn)
    return out.reshape(batch * seq, HW)


def _mem_kv_kernel(m_ref, g_ref, w_ref, gk_ref, k_ref, v_ref):
    m = m_ref[...]
    mn = m * lax.rsqrt(jnp.mean(m * m, axis=-1, keepdims=True) + EPS) * g_ref[...]
    kv = _dot(mn.astype(BF16), w_ref[...])
    for h in range(CA_HEADS):
        sl = slice(h * LANES, (h + 1) * LANES)
        kh = kv[:, sl]
        k_ref[:, sl] = (kh * lax.rsqrt(jnp.mean(kh * kh, axis=-1, keepdims=True) + EPS)
                        * gk_ref[...]).astype(k_ref.dtype)
    v_ref[...] = kv[:, CA_Q:].astype(v_ref.dtype)


def _mem_kv(mem2, g, w, gk, *, batch, n_mem):
    d = mem2.shape[1]
    return pl.pallas_call(
        _mem_kv_kernel,
        grid=(batch,),
        in_specs=[
            pl.BlockSpec((n_mem, d), lambda b: (b, 0)),
            pl.BlockSpec((1, d), lambda b: (0, 0)),
            pl.BlockSpec((d, 2 * CA_Q), lambda b: (0, 0)),
            pl.BlockSpec((1, LANES), lambda b: (0, 0)),
        ],
        out_specs=[
            pl.BlockSpec((n_mem, CA_Q), lambda b: (b, 0)),
            pl.BlockSpec((n_mem, CA_Q), lambda b: (b, 0)),
        ],
        out_shape=[
            jax.ShapeDtypeStruct((batch * n_mem, CA_Q), BF16),
            jax.ShapeDtypeStruct((batch * n_mem, CA_Q), BF16),
        ],
        compiler_params=_cparams(("parallel",)),
        name="mem_kv",
    )(mem2, g, w, gk)


def _back_kernel(x_ref, oa_ref, ob_ref, cq_ref, gt_ref, kc_ref, vc_ref, gcq_ref,
                 wa_ref, wb_ref, wc_ref, wo_ref, g2_ref, w1_ref, w2_ref, o_ref, *, tf):
    d = x_ref.shape[1]
    heads = []
    for h in range(CA_HEADS):
        sl = slice(h * LANES, (h + 1) * LANES)
        q = cq_ref[:, sl].astype(F32)
        qn = (q * lax.rsqrt(jnp.mean(q * q, axis=-1, keepdims=True) + EPS) * gcq_ref[...]
              * (CA_DIM ** -0.5))
        s = _nt_dot(qn.astype(BF16), kc_ref[:, sl])
        p = jnp.exp(s - jnp.max(s, axis=-1, keepdims=True))
        l = jnp.sum(p, axis=-1, keepdims=True)
        heads.append((_dot(p.astype(BF16), vc_ref[:, sl]) / l).astype(BF16))
    oc = jnp.concatenate(heads, axis=1)
    ya = _dot(oa_ref[...], wa_ref[...])
    yb = _dot(ob_ref[...], wb_ref[...])
    yc = _dot(oc, wc_ref[...])
    mixed = (gt_ref[:, 0:d].astype(F32) * ya + gt_ref[:, d:2 * d].astype(F32) * yb
             + gt_ref[:, 2 * d:3 * d].astype(F32) * yc)
    x1 = x_ref[...] + _dot(mixed.astype(BF16), wo_ref[...])
    ms = jnp.mean(x1 * x1, axis=-1, keepdims=True)
    h2 = (x1 * lax.rsqrt(ms + EPS) * g2_ref[...]).astype(BF16)
    o_ref[...] = x1
    for c0 in range(0, w1_ref.shape[1], tf):
        a = jnp.maximum(_dot(h2, w1_ref[:, c0:c0 + tf]), 0.0)
        o_ref[...] += _dot((a * a).astype(BF16), w2_ref[c0:c0 + tf, :])


def _back(x, oa, ob, proj, gates, kc, vc, gcq, wa, wb, wc, wo, g2, w1, w2, *, seq, n_mem, tm, tf):
    t, d = x.shape
    tiles_per_seq = seq // tm

    def tok(n, col=0):
        return pl.BlockSpec((tm, n), lambda i: (i, col))

    def const(a):
        return pl.BlockSpec(a.shape, lambda i: (0, 0), pipeline_mode=pl.Buffered(1))

    def mem_blk():
        return pl.BlockSpec((n_mem, CA_Q), lambda i: (i // tiles_per_seq, 0))

    return pl.pallas_call(
        functools.partial(_back_kernel, tf=tf),
        grid=(t // tm,),
        in_specs=[
            tok(d), tok(HW), tok(HW), tok(CA_Q, COL_CQ // CA_Q), tok(N_BRANCH * d),
            mem_blk(), mem_blk(), const(gcq), const(wa), const(wb), const(wc), const(wo), const(g2),
            const(w1), const(w2),
        ],
        out_specs=tok(d),
        out_shape=jax.ShapeDtypeStruct((t, d), F32),
        compiler_params=pltpu.CompilerParams(dimension_semantics=("parallel",),
                                             vmem_limit_bytes=BACK_VMEM_LIMIT),
        name="back",
    )(x, oa, ob, proj, gates, kc, vc, gcq, wa, wb, wc, wo, g2, w1, w2)


def _pick(n, pref):
    t = min(pref, n)
    while n % t:
        t //= 2
    return t


def _row(v, width=None):
    v = v.astype(F32).reshape(1, -1)
    if width is not None and v.shape[1] < width:
        v = jnp.pad(v, ((0, 0), (0, width - v.shape[1])))
    return v


def _layer(x2, mem2, pos2, inv_row, batch, seq, n_mem, l, p):
    t, d = x2.shape
    (ln1_g, w_in, w_gate, b_gate, da_qnorm_g, da_knorm_g, lq1, lk1, lq2, lk2, da_subln_g, w_o_diff,
     w_conv, gdn_a_log, gdn_dt_bias, gdn_norm_g, w_o_delta, mem_norm_g, w_mem_kv, ca_qnorm_g,
     ca_knorm_g, w_o_cross, w_out, ln2_g, w_mlp1, w_mlp2) = p

    o = [0]
    for n in (HW, HW, HW, HW, HW, HW, HW, GDN_HEADS, GDN_HEADS, CA_Q):
        o.append(o[-1] + n)
    w_t = w_in.T
    sec = [w_t[o[i]:o[i + 1]] for i in range(10)]
    w_ab = jnp.zeros((MXU_N, d), F32)
    w_ab = w_ab.at[AB_B0:AB_B0 + GDN_HEADS].set(sec[7]).at[AB_A0:AB_A0 + GDN_HEADS].set(sec[8])
    w_in_t = jnp.concatenate([sec[3], sec[4], sec[5], sec[6], sec[9], sec[2], sec[0], sec[1], w_ab],
                             axis=0).astype(BF16)
    gq = _row(jnp.tile(da_qnorm_g, 2))
    gk = _row(jnp.tile(da_knorm_g, 2))
    proj, ab, gates, qz, kr = _front(x2, _row(ln1_g), w_in_t, w_gate.astype(BF16), _row(b_gate), pos2,
                                     inv_row, gq, gk, tm=_pick(t, 512))

    lam_pack = jnp.zeros((8, LANES), F32)
    lam_pack = lam_pack.at[0:4, 0:DA_QK_DIM].set(jnp.stack([lq1, lk1, lq2, lk2]).astype(F32))
    lam_init = 0.8 - 0.6 * math.exp(-0.3 * l)
    oa = _diff_attn(qz, kr, proj, lam_pack, _row(da_subln_g), batch=batch, seq=seq,
                    tq=_pick(seq, 1024), tk=_pick(seq, 1024), seg=_pick(seq, 512), sub=256, ahead=2, unroll=2,
                    exp_dtype=BF16, lam_init=lam_init)

    alog_row = jnp.zeros((1, LANES), F32).at[0, AB_A0:AB_A0 + GDN_HEADS].set(gdn_a_log.astype(F32))
    dtb_row = jnp.zeros((1, LANES), F32).at[0, AB_A0:AB_A0 + GDN_HEADS].set(gdn_dt_bias.astype(F32))
    pq, cmat, cd, op = _gdn_prep(proj, ab, w_conv.astype(F32), alog_row, dtb_row,
                                 batch=batch, seq=seq, ts=_pick(seq, 256), cpb=4)
    ob = _gdn_scan(pq, cmat, cd, op, proj, _row(gdn_norm_g), batch=batch, seq=seq, ts=_pick(seq, 256))

    kc, vc = _mem_kv(mem2, _row(mem_norm_g), w_mem_kv.astype(BF16), _row(ca_knorm_g),
                     batch=batch, n_mem=n_mem)

    return _back(x2, oa, ob, proj, gates, kc, vc, _row(ca_qnorm_g), w_o_diff.astype(BF16),
                 w_o_delta.astype(BF16), w_o_cross.astype(BF16), w_out.astype(BF16), _row(ln2_g),
                 w_mlp1.astype(BF16), w_mlp2.astype(BF16), seq=seq, n_mem=n_mem,
                 tm=_pick(seq, 512), tf=1024)


def kernel(x, mem, positions, ln1_g, w_in, w_gate, b_gate, da_qnorm_g, da_knorm_g, da_lambda_q1, da_lambda_k1, da_lambda_q2, da_lambda_k2, da_subln_g, w_o_diff, w_conv, gdn_a_log, gdn_dt_bias, gdn_norm_g, w_o_delta, mem_norm_g, w_mem_kv, ca_qnorm_g, ca_knorm_g, w_o_cross, w_out, ln2_g, w_mlp1, w_mlp2):
    batch, seq, d = x.shape
    n_mem = mem.shape[1]
    x2 = x.reshape(batch * seq, d)
    mem2 = mem.reshape(batch * n_mem, d)
    pos2 = positions.reshape(batch * seq, 1)
    half = DA_QK_DIM // 2
    inv = jnp.exp(-math.log(ROPE_THETA) * jnp.arange(half, dtype=F32) / half)
    inv_row = jnp.tile(inv, LANES // half).reshape(1, LANES)
    params = (ln1_g, w_in, w_gate, b_gate, da_qnorm_g, da_knorm_g, da_lambda_q1, da_lambda_k1,
              da_lambda_q2, da_lambda_k2, da_subln_g, w_o_diff, w_conv, gdn_a_log, gdn_dt_bias,
              gdn_norm_g, w_o_delta, mem_norm_g, w_mem_kv, ca_qnorm_g, ca_knorm_g, w_o_cross, w_out,
              ln2_g, w_mlp1, w_mlp2)
    for l in range(ln1_g.shape[0]):
        x2 = _layer(x2, mem2, pos2, inv_row, batch, seq, n_mem, l, tuple(a[l] for a in params))
    return x2.reshape(batch, seq, d)
```

```python
import functools
import math

import jax
import jax.numpy as jnp
from jax import lax
from jax.experimental import pallas as pl
from jax.experimental.pallas import tpu as pltpu

F32 = jnp.float32
BF16 = jnp.bfloat16

EPS = 1e-6
CHUNK = 64
ROPE_THETA = 10000.0
N_BRANCH = 3

DA_HEADS = 6
DA_QK_DIM = 64
DA_V_DIM = 128
GDN_HEADS = 6
GDN_K_DIM = 128
GDN_V_DIM = 128
CONV_K = 4
CA_HEADS = 4
CA_DIM = 128

LANES = 128
HW = GDN_HEADS * GDN_K_DIM
CA_Q = CA_HEADS * CA_DIM

MXU_N = 256

COL_GQ, COL_GK, COL_GV, COL_GZ, COL_CQ, COL_DV = 0, HW, 2 * HW, 3 * HW, 4 * HW, 4 * HW + CA_Q
N_PROJ = 5 * HW + CA_Q
WCOL_DQ, WCOL_DK, WCOL_AB = N_PROJ, N_PROJ + HW, N_PROJ + 2 * HW
WCOL_GATE = WCOL_AB + MXU_N
AB_B0, AB_A0 = 0, 8

VMEM_LIMIT = 48 * 1024 * 1024
FRONT_VMEM_LIMIT = 56 * 1024 * 1024
BACK_VMEM_LIMIT = 58 * 1024 * 1024


def _cparams(sem):
    return pltpu.CompilerParams(dimension_semantics=sem, vmem_limit_bytes=VMEM_LIMIT)


def _nt_dot(a, b):
    return lax.dot_general(a, b, (((1,), (1,)), ((), ())), preferred_element_type=F32)


def _tn_dot(a, b):
    return lax.dot_general(a, b, (((0,), (0,)), ((), ())), preferred_element_type=F32)


def _dot(a, b):
    return jnp.dot(a, b, preferred_element_type=F32)


def _split_bf16(x):
    hi = x.astype(BF16)
    r = x - hi.astype(F32)
    mid = r.astype(BF16)
    lo = (r - mid.astype(F32)).astype(BF16)
    return hi, mid, lo


def _front_kernel(x_ref, g_ref, wt_ref, wg_ref, bg_ref, pos_ref, inv_ref, gq_ref, gk_ref,
                  proj_ref, ab_ref, gates_ref, qz_ref, kr_ref):
    tm = x_ref.shape[0]
    x = x_ref[...]
    h = (x * lax.rsqrt(jnp.mean(x * x, axis=-1, keepdims=True) + EPS) * g_ref[...]).astype(BF16)

    def cols(c0):
        return _nt_dot(h, wt_ref[c0:c0 + MXU_N, :])

    ang = pos_ref[...].astype(F32) * inv_ref[...]
    cos = jnp.cos(ang)
    sin = jnp.sin(ang)
    lane = lax.broadcasted_iota(jnp.int32, (tm, LANES), 1)
    lo_map = lane < DA_QK_DIM
    first_half = (lane % DA_QK_DIM) < (DA_QK_DIM // 2)
    sin_signed = jnp.where(first_half, -sin, sin)
    scale = DA_QK_DIM ** -0.5 * math.log2(math.e)

    def norm_rope(x, g):
        s = x * x
        tot = jnp.sum(s, axis=-1, keepdims=True)
        lo = jnp.sum(jnp.where(lo_map, s, 0.0), axis=-1, keepdims=True)
        ms = jnp.where(lo_map, lo, tot - lo) * (1.0 / DA_QK_DIM)
        y = x * lax.rsqrt(ms + EPS) * g
        partner = jnp.where(first_half,
                            pltpu.roll(y, LANES - DA_QK_DIM // 2, axis=1),
                            pltpu.roll(y, DA_QK_DIM // 2, axis=1))
        return y * cos + partner * sin_signed

    for c0 in range(0, HW, MXU_N):
        yq = cols(WCOL_DQ + c0)
        yk = cols(WCOL_DK + c0)
        for j in range(MXU_N // LANES):
            sl = slice(c0 + j * LANES, c0 + (j + 1) * LANES)
            q = norm_rope(yq[:, j * LANES:(j + 1) * LANES], gq_ref[...]) * scale
            qz_ref[0, :, sl] = jnp.where(lo_map, q, 0.0).astype(qz_ref.dtype)
            qz_ref[1, :, sl] = jnp.where(lo_map, 0.0, q).astype(qz_ref.dtype)
            k = norm_rope(yk[:, j * LANES:(j + 1) * LANES], gk_ref[...])
            kr_ref[:, sl] = k.astype(kr_ref.dtype)

    ab_ref[...] = cols(WCOL_AB)[:, :LANES]

    for c0 in range(0, gates_ref.shape[1], MXU_N):
        y = _dot(h, wg_ref[:, c0:c0 + MXU_N]) + bg_ref[:, c0:c0 + MXU_N]
        gates_ref[:, c0:c0 + MXU_N] = jax.nn.sigmoid(y).astype(gates_ref.dtype)

    for c0 in range(0, N_PROJ, MXU_N):
        proj_ref[:, c0:c0 + MXU_N] = cols(c0).astype(proj_ref.dtype)


def _front(x, g, w_in_t, w_gate, b_gate, pos, inv, gq, gk, *, tm):
    t, d = x.shape
    n_gate = b_gate.shape[1]

    def tok(n):
        return pl.BlockSpec((tm, n), lambda i: (i, 0))

    def const(shape, **kw):
        return pl.BlockSpec(shape, lambda i: (0,) * len(shape), **kw)

    return pl.pallas_call(
        _front_kernel,
        grid=(t // tm,),
        in_specs=[
            tok(d), const((1, d)),
            const(w_in_t.shape, pipeline_mode=pl.Buffered(1)),
            const(w_gate.shape, pipeline_mode=pl.Buffered(1)),
            const((1, n_gate)), tok(1), const((1, LANES)), const((1, LANES)), const((1, LANES)),
        ],
        out_specs=[
            tok(N_PROJ), tok(LANES), tok(n_gate),
            pl.BlockSpec((2, tm, HW), lambda i: (0, i, 0)),
            tok(HW),
        ],
        out_shape=[
            jax.ShapeDtypeStruct((t, N_PROJ), BF16),
            jax.ShapeDtypeStruct((t, LANES), F32),
            jax.ShapeDtypeStruct((t, n_gate), BF16),
            jax.ShapeDtypeStruct((2, t, HW), BF16),
            jax.ShapeDtypeStruct((t, HW), BF16),
        ],
        compiler_params=pltpu.CompilerParams(dimension_semantics=("parallel",),
                                             vmem_limit_bytes=FRONT_VMEM_LIMIT),
        name="front",
    )(x, g, w_in_t, w_gate, b_gate, pos, inv, gq, gk)


def _diff_attn_kernel(qz_ref, k_ref, v_ref, lam_ref, g_ref, o_ref, m_ref, acc_ref, v1_ref,
                      *, tq, tk, seg, sub, ahead, unroll, exp_dtype, lam_init):
    i = pl.program_id(2)
    n_sub = 2 * tq // sub

    @pl.when(i == 0)
    def _():
        v1_ref[:, 0:LANES] = v_ref[...]
        v1_ref[:, LANES:] = jnp.ones((v1_ref.shape[0], LANES), v1_ref.dtype)

    def scores(base, item):
        r, k0, nk = item[:3]
        mp, off = divmod(r * sub, tq)
        return _nt_dot(qz_ref[mp, off:off + sub, :], k_ref[pl.ds(base + k0, nk), :])

    def consume(base, item, s):
        r, k0, nk, masked, first = item
        off = (r * sub) % tq
        rs = slice(r * sub, (r + 1) * sub)
        if masked:
            row = lax.broadcasted_iota(jnp.int32, s.shape, 0) + off
            col = lax.broadcasted_iota(jnp.int32, s.shape, 1) + k0
            s = jnp.where((col // CHUNK) <= (row // CHUNK), s, -jnp.inf)
        m_cur = jnp.max(s, axis=-1, keepdims=True)
        m_new = jnp.broadcast_to(m_cur, (sub, LANES)) if first else jnp.maximum(m_ref[rs], m_cur)
        x = s - jnp.concatenate([m_new] * (nk // LANES), axis=1)
        p = jnp.exp2(x.astype(exp_dtype)).astype(BF16)
        pv = _dot(p, v1_ref[pl.ds(base + k0, nk), :])
        if first:
            acc_ref[rs] = pv
        else:
            alpha = jnp.exp2(m_ref[rs] - m_new)
            acc_ref[rs] = jnp.concatenate([alpha, alpha], axis=1) * acc_ref[rs] + pv
        m_ref[rs] = m_new

    def pipelined(base, items):
        pending = [scores(base, it) for it in items[:ahead]]
        for n, it in enumerate(items):
            s = pending.pop(0)
            if n + ahead < len(items):
                pending.append(scores(base, items[n + ahead]))
            consume(base, it, s)

    diag_items = []
    for kb in range(tq // seg):
        for r in range(n_sub):
            off = (r * sub) % tq
            k0, k1 = kb * seg, min((kb + 1) * seg, off + sub)
            if k1 > k0:
                diag_items.append((r, k0, k1 - k0, k1 > off, kb == 0))
    pipelined(pl.multiple_of(i * tq, tq), diag_items)

    def full_items(n_blocks):
        return [(r, kb * seg, seg, False, False)
                for kb in range(n_blocks * tk // seg) for r in range(n_sub)]

    n_full = i * (tq // tk)

    def body(j, carry):
        pipelined(pl.multiple_of(j * (unroll * tk), unroll * tk), full_items(unroll))
        return carry

    lax.fori_loop(0, n_full // unroll, body, 0)

    group = unroll // 2
    while group >= 1:
        done = (n_full // (2 * group)) * (2 * group)

        @pl.when(n_full % (2 * group) >= group)
        def _(group=group, done=done):
            pipelined(pl.multiple_of(done * tk, tk), full_items(group))

        group //= 2

    o = acc_ref[:, 0:LANES] / acc_ref[:, LANES:]
    lp = lam_ref[...]
    lam = (jnp.exp(jnp.sum(lp[0:1] * lp[1:2], axis=-1, keepdims=True))
           - jnp.exp(jnp.sum(lp[2:3] * lp[3:4], axis=-1, keepdims=True)) + lam_init)
    d = o[:tq] - lam * o[tq:]
    ms = jnp.mean(d * d, axis=-1, keepdims=True)
    o_ref[...] = (d * lax.rsqrt(ms + EPS) * g_ref[...] * (1.0 - lam_init)).astype(o_ref.dtype)


def _diff_attn(qz, kr, proj, lam_pack, g, *, batch, seq, tq, tk, seg, sub, ahead, unroll, exp_dtype,
               lam_init):
    t = batch * seq
    nq = seq // tq
    return pl.pallas_call(
        functools.partial(_diff_attn_kernel, tq=tq, tk=tk, seg=seg, sub=sub, ahead=ahead, unroll=unroll,
                          exp_dtype=exp_dtype, lam_init=lam_init),
        grid=(batch, DA_HEADS, nq),
        in_specs=[
            pl.BlockSpec((2, tq, LANES), lambda b, h, i: (0, b * nq + i, h)),
            pl.BlockSpec((seq, LANES), lambda b, h, i: (b, h)),
            pl.BlockSpec((seq, LANES), lambda b, h, i: (b, COL_DV // LANES + h)),
            pl.BlockSpec((8, LANES), lambda b, h, i: (0, 0)),
            pl.BlockSpec((1, LANES), lambda b, h, i: (0, 0)),
        ],
        out_specs=pl.BlockSpec((tq, LANES), lambda b, h, i: (b * nq + i, h)),
        out_shape=jax.ShapeDtypeStruct((t, HW), BF16),
        scratch_shapes=[
            pltpu.VMEM((2 * tq, LANES), F32),
            pltpu.VMEM((2 * tq, 2 * LANES), F32),
            pltpu.VMEM((seq, 2 * LANES), BF16),
        ],
        compiler_params=_cparams(("parallel", "parallel", "arbitrary")),
        name="diff_attn",
    )(qz, kr, proj, lam_pack, g)


def _gdn_prep_kernel(q_ref, k_ref, v_ref, qh_ref, kh_ref, vh_ref, ab_ref, wc_ref, alog_ref, dtb_ref,
                     pq_ref, c_ref, cd_ref, op_ref,
                     xp_ref, qn_ref, kn_ref, vn_ref, beta_ref, g_ref, grow_ref, *, ts, cpb):
    first = pl.program_id(1) == 0
    nc = ts // CHUNK

    for s, (cur_ref, halo_ref, dst_ref) in enumerate(
            ((q_ref, qh_ref, qn_ref), (k_ref, kh_ref, kn_ref), (v_ref, vh_ref, vn_ref))):
        halo = halo_ref[...].astype(F32)
        xp_ref[0:8, :] = jnp.where(first, 0.0, halo)
        xp_ref[8:, :] = cur_ref[...].astype(F32)
        acc = xp_ref[8:8 + ts, :] * wc_ref[CONV_K - 1:CONV_K, s * HW:(s + 1) * HW]
        for j in range(1, CONV_K):
            acc = acc + xp_ref[8 - j:8 - j + ts, :] * wc_ref[CONV_K - 1 - j:CONV_K - j, s * HW:(s + 1) * HW]
        half = 0.5 * acc
        act = half + half * jnp.tanh(half)
        if s < 2:
            for h in range(GDN_HEADS):
                sl = slice(h * LANES, (h + 1) * LANES)
                a_h = act[:, sl]
                dst_ref[:, sl] = a_h * lax.rsqrt(jnp.sum(a_h * a_h, axis=-1, keepdims=True) + EPS)
        else:
            dst_ref[...] = act

    ab = ab_ref[...]
    beta_ref[...] = jax.nn.sigmoid(ab)
    xa = ab + dtb_ref[...]
    softplus = jnp.maximum(xa, 0.0) + jnp.log1p(jnp.exp(-jnp.abs(xa)))
    la = -jnp.exp(alog_ref[...]) * softplus
    ti = lax.broadcasted_iota(jnp.int32, (ts, ts), 0)
    tj = lax.broadcasted_iota(jnp.int32, (ts, ts), 1)
    ltri = jnp.where((tj <= ti) & ((ti // CHUNK) == (tj // CHUNK)), 1.0, 0.0).astype(BF16)
    g_all = sum(_dot(ltri, part) for part in _split_bf16(la))
    g_ref[...] = g_all
    sel_r = lax.broadcasted_iota(jnp.int32, (8, LANES), 0)
    sel_c = lax.broadcasted_iota(jnp.int32, (8, LANES), 1)
    sel = jnp.where(sel_c == sel_r + AB_A0, 1.0, 0.0).astype(BF16)
    grow_all = sum(_nt_dot(sel, part) for part in _split_bf16(g_all))
    for c in range(nc):
        grow_ref[c] = grow_all[:, c * CHUNK:(c + 1) * CHUNK]

    ri = lax.broadcasted_iota(jnp.int32, (CHUNK, CHUNK), 0)
    ci = lax.broadcasted_iota(jnp.int32, (CHUNK, CHUNK), 1)
    incl = ci <= ri
    strict = ci < ri
    scale = GDN_K_DIM ** -0.5

    eye = jnp.where(ri == ci, 1.0, 0.0)
    same_blk = (ri // 16) == (ci // 16)

    def bdot(a, b):
        return _dot(a.astype(BF16), b.astype(BF16))

    def group_body(cg, carry):
        us = []
        for cc in range(cpb):
            c = cg * cpb + cc
            r0 = pl.multiple_of(c * CHUNK, CHUNK)
            rows = pl.ds(r0, CHUNK)
            for h in range(GDN_HEADS):
                sl = slice(h * LANES, (h + 1) * LANES)
                gcol = g_ref[rows, AB_A0 + h:AB_A0 + h + 1]
                glast = g_ref[pl.ds(r0 + CHUNK - 1, 1), AB_A0 + h:AB_A0 + h + 1]
                grow = grow_ref[c, h:h + 1, :]
                us.append(dict(
                    c=c, h=h, rows=rows, sl=sl, gcol=gcol, glast=glast,
                    q=qn_ref[rows, sl] * scale, k=kn_ref[rows, sl], v=vn_ref[rows, sl],
                    b=beta_ref[rows, AB_B0 + h:AB_B0 + h + 1], eg=jnp.exp(gcol),
                    decay=jnp.where(incl, jnp.exp(jnp.where(incl, gcol - grow, 0.0)), 0.0)))

        for u in us:
            kb = u["k"].astype(BF16)
            u["qk_kk"] = _nt_dot(jnp.concatenate([u["q"].astype(BF16), kb], axis=0), kb)
        for u in us:
            a = jnp.where(strict, u["b"] * u["qk_kk"][CHUNK:] * u["decay"], 0.0)
            ad = jnp.where(same_blk, a, 0.0)
            u["ao"] = a - ad
            u["x"] = eye - ad
            u["ad"] = ad
        for u in us:
            u["p"] = bdot(u["ad"], u["ad"])
        for _ in range(2):
            for u in us:
                xp = bdot(u["x"], u["p"])
                u["p"] = bdot(u["p"], u["p"])
                u["x"] = u["x"] + xp
        for u in us:
            u["x"] = u["x"] + bdot(u["x"], u["p"])
        for u in us:
            rhs = jnp.concatenate([u["k"] * (u["b"] * u["eg"]), u["v"] * u["b"]], axis=1)
            u["bm"] = bdot(u["x"], u["ao"])
            u["z"] = bdot(u["x"], rhs)
        for u in us:
            u["b2"] = bdot(u["bm"], u["bm"])
        for u in us:
            u["z"] = u["z"] + bdot(u["b2"], u["z"])
        for u in us:
            u["wu"] = (u["z"] - bdot(u["bm"], u["z"])).astype(BF16)
        for u in us:
            qkm = (u["qk_kk"][:CHUNK] * u["decay"]).astype(BF16)
            kd = (u["k"] * jnp.exp(u["glast"] - u["gcol"])).astype(BF16)
            u["r2"] = _dot(qkm, u["wu"])
            u["r3"] = _tn_dot(kd, u["wu"])
        for u in us:
            c, h = u["c"], u["h"]
            pq_ref[c, h, 0:LANES, :] = u["r3"][:, :LANES].astype(pq_ref.dtype)
            pq_ref[c, h, LANES:, :] = (u["q"] * u["eg"] - u["r2"][:, :LANES]).astype(pq_ref.dtype)
            c_ref[c, h] = u["r3"][:, LANES:].astype(c_ref.dtype)
            cd_ref[c, h] = jnp.broadcast_to(jnp.exp(u["glast"]), (8, LANES))
            op_ref[u["rows"], u["sl"]] = u["r2"][:, LANES:].astype(op_ref.dtype)
        return carry

    lax.fori_loop(0, nc // cpb, group_body, 0)


def _gdn_prep(proj, ab, wconv, alog_row, dtb_row, *, batch, seq, ts, cpb):
    t = batch * seq
    nt = seq // ts
    nc = ts // CHUNK
    n_chunks = t // CHUNK
    hb = ts // 8

    def cur(col):
        return pl.BlockSpec((ts, HW), lambda b, i: (b * nt + i, col // HW))

    def halo(col):
        return pl.BlockSpec((8, HW), lambda b, i: (jnp.maximum((b * nt + i) * hb - 1, 0), col // HW))

    return pl.pallas_call(
        functools.partial(_gdn_prep_kernel, ts=ts, cpb=cpb),
        grid=(batch, nt),
        in_specs=[
            cur(COL_GQ), cur(COL_GK), cur(COL_GV), halo(COL_GQ), halo(COL_GK), halo(COL_GV),
            pl.BlockSpec((ts, LANES), lambda b, i: (b * nt + i, 0)),
            pl.BlockSpec((CONV_K, 3 * HW), lambda b, i: (0, 0)),
            pl.BlockSpec((1, LANES), lambda b, i: (0, 0)),
            pl.BlockSpec((1, LANES), lambda b, i: (0, 0)),
        ],
        out_specs=[
            pl.BlockSpec((nc, GDN_HEADS, GDN_K_DIM + CHUNK, LANES), lambda b, i: (b * nt + i, 0, 0, 0)),
            pl.BlockSpec((nc, GDN_HEADS, GDN_K_DIM, LANES), lambda b, i: (b * nt + i, 0, 0, 0)),
            pl.BlockSpec((nc, GDN_HEADS, 8, LANES), lambda b, i: (b * nt + i, 0, 0, 0)),
            pl.BlockSpec((ts, HW), lambda b, i: (b * nt + i, 0)),
        ],
        out_shape=[
            jax.ShapeDtypeStruct((n_chunks, GDN_HEADS, GDN_K_DIM + CHUNK, LANES), BF16),
            jax.ShapeDtypeStruct((n_chunks, GDN_HEADS, GDN_K_DIM, LANES), BF16),
            jax.ShapeDtypeStruct((n_chunks, GDN_HEADS, 8, LANES), F32),
            jax.ShapeDtypeStruct((t, HW), BF16),
        ],
        scratch_shapes=[
            pltpu.VMEM((ts + 8, HW), F32),
            pltpu.VMEM((ts, HW), F32),
            pltpu.VMEM((ts, HW), F32),
            pltpu.VMEM((ts, HW), F32),
            pltpu.VMEM((ts, LANES), F32),
            pltpu.VMEM((ts, LANES), F32),
            pltpu.VMEM((nc, 8, CHUNK), F32),
        ],
        compiler_params=_cparams(("parallel", "parallel")),
        name="gdn_prep",
    )(proj, proj, proj, proj, proj, proj, ab, wconv, alog_row, dtb_row)


def _gdn_scan_kernel(pq_ref, c_ref, cd_ref, op_ref, z_ref, gn_ref, o_ref, s_ref, *, ts):
    @pl.when(pl.program_id(0) == 0)
    def _():
        s_ref[...] = jnp.zeros(s_ref.shape, F32)

    nc = ts // CHUNK
    batch = s_ref.shape[0]

    def chunk_body(c, carry):
        r0 = pl.multiple_of(c * CHUNK, CHUNK)
        rows = pl.ds(r0, CHUNK)
        for b in range(batch):
            for h in range(GDN_HEADS):
                sl = slice(h * LANES, (h + 1) * LANES)
                s = s_ref[b, h]
                r = _dot(pq_ref[b, c, h], s.astype(BF16))
                o = r[GDN_K_DIM:] + op_ref[b, rows, sl].astype(F32)
                s_ref[b, h] = cd_ref[b, c, h, 0:1, :] * s - r[:GDN_K_DIM] + c_ref[b, c, h].astype(F32)
                on = o * lax.rsqrt(jnp.mean(o * o, axis=-1, keepdims=True) + EPS) * gn_ref[...]
                z = z_ref[b, rows, sl].astype(F32)
                o_ref[b, rows, sl] = (on * (z * jax.nn.sigmoid(z))).astype(o_ref.dtype)
        return carry

    lax.fori_loop(0, nc, chunk_body, 0)


def _gdn_scan(pq, cmat, cd, op, proj, gn, *, batch, seq, ts):
    nt = seq // ts
    nc = ts // CHUNK
    n = seq // CHUNK

    def per_chunk(a):
        return a.reshape((batch, n) + a.shape[1:])

    def chunk_blk(rows):
        return pl.BlockSpec((batch, nc, GDN_HEADS, rows, LANES), lambda i: (0, i, 0, 0, 0))

    def tok_blk(col):
        return pl.BlockSpec((batch, ts, HW), lambda i: (0, i, col))

    out = pl.pallas_call(
        functools.partial(_gdn_scan_kernel, ts=ts),
        grid=(nt,),
        in_specs=[
            chunk_blk(GDN_K_DIM + CHUNK), chunk_blk(GDN_K_DIM), chunk_blk(8),
            tok_blk(0), tok_blk(COL_GZ // HW),
            pl.BlockSpec((1, LANES), lambda i: (0, 0)),
        ],
        out_specs=tok_blk(0),
        out_shape=jax.ShapeDtypeStruct((batch, seq, HW), BF16),
        scratch_shapes=[pltpu.VMEM((batch, GDN_HEADS, GDN_K_DIM, GDN_V_DIM), F32)],
        compiler_params=_cparams(("arbitrary",)),
        name="gdn_scan",
    )(per_chunk(pq), per_chunk(cmat), per_chunk(cd), op.reshape(batch, seq, HW),
      proj.reshape(batch, seq, proj.shape[1]), gn)
    return out.reshape(batch * seq, HW)


def _mem_kv_kernel(m_ref, g_ref, w_ref, gk_ref, k_ref, v_ref):
    m = m_ref[...]
    mn = m * lax.rsqrt(jnp.mean(m * m, axis=-1, keepdims=True) + EPS) * g_ref[...]
    kv = _dot(mn.astype(BF16), w_ref[...])
    for h in range(CA_HEADS):
        sl = slice(h * LANES, (h + 1) * LANES)
        kh = kv[:, sl]
        k_ref[:, sl] = (kh * lax.rsqrt(jnp.mean(kh * kh, axis=-1, keepdims=True) + EPS)
                        * gk_ref[...]).astype(k_ref.dtype)
    v_ref[...] = kv[:, CA_Q:].astype(v_ref.dtype)


def _mem_kv(mem2, g, w, gk, *, batch, n_mem):
    d = mem2.shape[1]
    return pl.pallas_call(
        _mem_kv_kernel,
        grid=(batch,),
        in_specs=[
            pl.BlockSpec((n_mem, d), lambda b: (b, 0)),
            pl.BlockSpec((1, d), lambda b: (0, 0)),
            pl.BlockSpec((d, 2 * CA_Q), lambda b: (0, 0)),
            pl.BlockSpec((1, LANES), lambda b: (0, 0)),
        ],
        out_specs=[
            pl.BlockSpec((n_mem, CA_Q), lambda b: (b, 0)),
            pl.BlockSpec((n_mem, CA_Q), lambda b: (b, 0)),
        ],
        out_shape=[
            jax.ShapeDtypeStruct((batch * n_mem, CA_Q), BF16),
            jax.ShapeDtypeStruct((batch * n_mem, CA_Q), BF16),
        ],
        compiler_params=_cparams(("parallel",)),
        name="mem_kv",
    )(mem2, g, w, gk)


def _back_kernel(x_ref, oa_ref, ob_ref, cq_ref, gt_ref, kc_ref, vc_ref, gcq_ref,
                 wa_ref, wb_ref, wc_ref, wo_ref, g2_ref, w1_ref, w2_ref, o_ref, *, tf):
    d = x_ref.shape[1]
    heads = []
    for h in range(CA_HEADS):
        sl = slice(h * LANES, (h + 1) * LANES)
        q = cq_ref[:, sl].astype(F32)
        qn = (q * lax.rsqrt(jnp.mean(q * q, axis=-1, keepdims=True) + EPS) * gcq_ref[...]
              * (CA_DIM ** -0.5))
        s = _nt_dot(qn.astype(BF16), kc_ref[:, sl])
        p = jnp.exp(s - jnp.max(s, axis=-1, keepdims=True))
        l = jnp.sum(p, axis=-1, keepdims=True)
        heads.append((_dot(p.astype(BF16), vc_ref[:, sl]) / l).astype(BF16))
    oc = jnp.concatenate(heads, axis=1)
    ya = _dot(oa_ref[...], wa_ref[...])
    yb = _dot(ob_ref[...], wb_ref[...])
    yc = _dot(oc, wc_ref[...])
    mixed = (gt_ref[:, 0:d].astype(F32) * ya + gt_ref[:, d:2 * d].astype(F32) * yb
             + gt_ref[:, 2 * d:3 * d].astype(F32) * yc)
    x1 = x_ref[...] + _dot(mixed.astype(BF16), wo_ref[...])
    ms = jnp.mean(x1 * x1, axis=-1, keepdims=True)
    h2 = (x1 * lax.rsqrt(ms + EPS) * g2_ref[...]).astype(BF16)
    o_ref[...] = x1
    for c0 in range(0, w1_ref.shape[1], tf):
        a = jnp.maximum(_dot(h2, w1_ref[:, c0:c0 + tf]), 0.0)
        o_ref[...] += _dot((a * a).astype(BF16), w2_ref[c0:c0 + tf, :])


def _back(x, oa, ob, proj, gates, kc, vc, gcq, wa, wb, wc, wo, g2, w1, w2, *, seq, n_mem, tm, tf):
    t, d = x.shape
    tiles_per_seq = seq // tm

    def tok(n, col=0):
        return pl.BlockSpec((tm, n), lambda i: (i, col))

    def const(a):
        return pl.BlockSpec(a.shape, lambda i: (0, 0), pipeline_mode=pl.Buffered(1))

    def mem_blk():
        return pl.BlockSpec((n_mem, CA_Q), lambda i: (i // tiles_per_seq, 0))

    return pl.pallas_call(
        functools.partial(_back_kernel, tf=tf),
        grid=(t // tm,),
        in_specs=[
            tok(d), tok(HW), tok(HW), tok(CA_Q, COL_CQ // CA_Q), tok(N_BRANCH * d),
            mem_blk(), mem_blk(), const(gcq), const(wa), const(wb), const(wc), const(wo), const(g2),
            const(w1), const(w2),
        ],
        out_specs=tok(d),
        out_shape=jax.ShapeDtypeStruct((t, d), F32),
        compiler_params=pltpu.CompilerParams(dimension_semantics=("parallel",),
                                             vmem_limit_bytes=BACK_VMEM_LIMIT),
        name="back",
    )(x, oa, ob, proj, gates, kc, vc, gcq, wa, wb, wc, wo, g2, w1, w2)


def _pick(n, pref):
    t = min(pref, n)
    while n % t:
        t //= 2
    return t


def _row(v, width=None):
    v = v.astype(F32).reshape(1, -1)
    if width is not None and v.shape[1] < width:
        v = jnp.pad(v, ((0, 0), (0, width - v.shape[1])))
    return v


def _layer(x2, mem2, pos2, inv_row, batch, seq, n_mem, l, p):
    t, d = x2.shape
    (ln1_g, w_in, w_gate, b_gate, da_qnorm_g, da_knorm_g, lq1, lk1, lq2, lk2, da_subln_g, w_o_diff,
     w_conv, gdn_a_log, gdn_dt_bias, gdn_norm_g, w_o_delta, mem_norm_g, w_mem_kv, ca_qnorm_g,
     ca_knorm_g, w_o_cross, w_out, ln2_g, w_mlp1, w_mlp2) = p

    o = [0]
    for n in (HW, HW, HW, HW, HW, HW, HW, GDN_HEADS, GDN_HEADS, CA_Q):
        o.append(o[-1] + n)
    w_t = w_in.T
    sec = [w_t[o[i]:o[i + 1]] for i in range(10)]
    w_ab = jnp.zeros((MXU_N, d), F32)
    w_ab = w_ab.at[AB_B0:AB_B0 + GDN_HEADS].set(sec[7]).at[AB_A0:AB_A0 + GDN_HEADS].set(sec[8])
    w_in_t = jnp.concatenate([sec[3], sec[4], sec[5], sec[6], sec[9], sec[2], sec[0], sec[1], w_ab],
                             axis=0).astype(BF16)
    gq = _row(jnp.tile(da_qnorm_g, 2))
    gk = _row(jnp.tile(da_knorm_g, 2))
    proj, ab, gates, qz, kr = _front(x2, _row(ln1_g), w_in_t, w_gate.astype(BF16), _row(b_gate), pos2,
                                     inv_row, gq, gk, tm=_pick(t, 512))

    lam_pack = jnp.zeros((8, LANES), F32)
    lam_pack = lam_pack.at[0:4, 0:DA_QK_DIM].set(jnp.stack([lq1, lk1, lq2, lk2]).astype(F32))
    lam_init = 0.8 - 0.6 * math.exp(-0.3 * l)
    oa = _diff_attn(qz, kr, proj, lam_pack, _row(da_subln_g), batch=batch, seq=seq,
                    tq=_pick(seq, 1024), tk=_pick(seq, 1024), seg=_pick(seq, 512), sub=256, ahead=3, unroll=4,
                    exp_dtype=BF16, lam_init=lam_init)

    alog_row = jnp.zeros((1, LANES), F32).at[0, AB_A0:AB_A0 + GDN_HEADS].set(gdn_a_log.astype(F32))
    dtb_row = jnp.zeros((1, LANES), F32).at[0, AB_A0:AB_A0 + GDN_HEADS].set(gdn_dt_bias.astype(F32))
    pq, cmat, cd, op = _gdn_prep(proj, ab, w_conv.astype(F32), alog_row, dtb_row,
                                 batch=batch, seq=seq, ts=_pick(seq, 256), cpb=4)
    ob = _gdn_scan(pq, cmat, cd, op, proj, _row(gdn_norm_g), batch=batch, seq=seq, ts=_pick(seq, 256))

    kc, vc = _mem_kv(mem2, _row(mem_norm_g), w_mem_kv.astype(BF16), _row(ca_knorm_g),
                     batch=batch, n_mem=n_mem)

    return _back(x2, oa, ob, proj, gates, kc, vc, _row(ca_qnorm_g), w_o_diff.astype(BF16),
                 w_o_delta.astype(BF16), w_o_cross.astype(BF16), w_out.astype(BF16), _row(ln2_g),
                 w_mlp1.astype(BF16), w_mlp2.astype(BF16), seq=seq, n_mem=n_mem,
                 tm=_pick(seq, 512), tf=1024)


def kernel(x, mem, positions, ln1_g, w_in, w_gate, b_gate, da_qnorm_g, da_knorm_g, da_lambda_q1, da_lambda_k1, da_lambda_q2, da_lambda_k2, da_subln_g, w_o_diff, w_conv, gdn_a_log, gdn_dt_bias, gdn_norm_g, w_o_delta, mem_norm_g, w_mem_kv, ca_qnorm_g, ca_knorm_g, w_o_cross, w_out, ln2_g, w_mlp1, w_mlp2):
    batch, seq, d = x.shape
    n_mem = mem.shape[1]
    x2 = x.reshape(batch * seq, d)
    mem2 = mem.reshape(batch * n_mem, d)
    pos2 = positions.reshape(batch * seq, 1)
    half = DA_QK_DIM // 2
    inv = jnp.exp(-math.log(ROPE_THETA) * jnp.arange(half, dtype=F32) / half)
    inv_row = jnp.tile(inv, LANES // half).reshape(1, LANES)
    params = (ln1_g, w_in, w_gate, b_gate, da_qnorm_g, da_knorm_g, da_lambda_q1, da_lambda_k1,
              da_lambda_q2, da_lambda_k2, da_subln_g, w_o_diff, w_conv, gdn_a_log, gdn_dt_bias,
              gdn_norm_g, w_o_delta, mem_norm_g, w_mem_kv, ca_qnorm_g, ca_knorm_g, w_o_cross, w_out,
              ln2_g, w_mlp1, w_mlp2)
    for l in range(ln1_g.shape[0]):
        x2 = _layer(x2, mem2, pos2, inv_row, batch, seq, n_mem, l, tuple(a[l] for a in params))
    return x2.reshape(batch, seq, d)
```

```python
import functools
import math

import jax
import jax.numpy as jnp
from jax import lax
from jax.experimental import pallas as pl
from jax.experimental.pallas import tpu as pltpu

F32 = jnp.float32
BF16 = jnp.bfloat16

EPS = 1e-6
CHUNK = 64
ROPE_THETA = 10000.0
N_BRANCH = 3

DA_HEADS = 6
DA_QK_DIM = 64
DA_V_DIM = 128
GDN_HEADS = 6
GDN_K_DIM = 128
GDN_V_DIM = 128
CONV_K = 4
CA_HEADS = 4
CA_DIM = 128

LANES = 128
HW = GDN_HEADS * GDN_K_DIM
CA_Q = CA_HEADS * CA_DIM

MXU_N = 256

COL_GQ, COL_GK, COL_GV, COL_GZ, COL_CQ, COL_DV = 0, HW, 2 * HW, 3 * HW, 4 * HW, 4 * HW + CA_Q
N_PROJ = 5 * HW + CA_Q
WCOL_DQ, WCOL_DK, WCOL_AB = N_PROJ, N_PROJ + HW, N_PROJ + 2 * HW
WCOL_GATE = WCOL_AB + MXU_N
AB_B0, AB_A0 = 0, 8

VMEM_LIMIT = 48 * 1024 * 1024
FRONT_VMEM_LIMIT = 56 * 1024 * 1024
BACK_VMEM_LIMIT = 58 * 1024 * 1024


def _cparams(sem):
    return pltpu.CompilerParams(dimension_semantics=sem, vmem_limit_bytes=VMEM_LIMIT)


def _nt_dot(a, b):
    return lax.dot_general(a, b, (((1,), (1,)), ((), ())), preferred_element_type=F32)


def _tn_dot(a, b):
    return lax.dot_general(a, b, (((0,), (0,)), ((), ())), preferred_element_type=F32)


def _dot(a, b):
    return jnp.dot(a, b, preferred_element_type=F32)


def _split_bf16(x):
    hi = x.astype(BF16)
    r = x - hi.astype(F32)
    mid = r.astype(BF16)
    lo = (r - mid.astype(F32)).astype(BF16)
    return hi, mid, lo


def _front_kernel(x_ref, g_ref, wt_ref, wg_ref, bg_ref, pos_ref, inv_ref, gq_ref, gk_ref,
                  proj_ref, ab_ref, gates_ref, qz_ref, kr_ref):
    tm = x_ref.shape[0]
    x = x_ref[...]
    h = (x * lax.rsqrt(jnp.mean(x * x, axis=-1, keepdims=True) + EPS) * g_ref[...]).astype(BF16)

    def cols(c0):
        return _nt_dot(h, wt_ref[c0:c0 + MXU_N, :])

    ang = pos_ref[...].astype(F32) * inv_ref[...]
    cos = jnp.cos(ang)
    sin = jnp.sin(ang)
    lane = lax.broadcasted_iota(jnp.int32, (tm, LANES), 1)
    lo_map = lane < DA_QK_DIM
    first_half = (lane % DA_QK_DIM) < (DA_QK_DIM // 2)
    sin_signed = jnp.where(first_half, -sin, sin)
    scale = DA_QK_DIM ** -0.5 * math.log2(math.e)

    def norm_rope(x, g):
        s = x * x
        tot = jnp.sum(s, axis=-1, keepdims=True)
        lo = jnp.sum(jnp.where(lo_map, s, 0.0), axis=-1, keepdims=True)
        ms = jnp.where(lo_map, lo, tot - lo) * (1.0 / DA_QK_DIM)
        y = x * lax.rsqrt(ms + EPS) * g
        partner = jnp.where(first_half,
                            pltpu.roll(y, LANES - DA_QK_DIM // 2, axis=1),
                            pltpu.roll(y, DA_QK_DIM // 2, axis=1))
        return y * cos + partner * sin_signed

    for c0 in range(0, HW, MXU_N):
        yq = cols(WCOL_DQ + c0)
        yk = cols(WCOL_DK + c0)
        for j in range(MXU_N // LANES):
            sl = slice(c0 + j * LANES, c0 + (j + 1) * LANES)
            q = norm_rope(yq[:, j * LANES:(j + 1) * LANES], gq_ref[...]) * scale
            qz_ref[0, :, sl] = jnp.where(lo_map, q, 0.0).astype(qz_ref.dtype)
            qz_ref[1, :, sl] = jnp.where(lo_map, 0.0, q).astype(qz_ref.dtype)
            k = norm_rope(yk[:, j * LANES:(j + 1) * LANES], gk_ref[...])
            kr_ref[:, sl] = k.astype(kr_ref.dtype)

    ab_ref[...] = cols(WCOL_AB)[:, :LANES]

    for c0 in range(0, gates_ref.shape[1], MXU_N):
        y = _dot(h, wg_ref[:, c0:c0 + MXU_N]) + bg_ref[:, c0:c0 + MXU_N]
        gates_ref[:, c0:c0 + MXU_N] = jax.nn.sigmoid(y).astype(gates_ref.dtype)

    for c0 in range(0, N_PROJ, MXU_N):
        proj_ref[:, c0:c0 + MXU_N] = cols(c0).astype(proj_ref.dtype)


def _front(x, g, w_in_t, w_gate, b_gate, pos, inv, gq, gk, *, tm):
    t, d = x.shape
    n_gate = b_gate.shape[1]

    def tok(n):
        return pl.BlockSpec((tm, n), lambda i: (i, 0))

    def const(shape, **kw):
        return pl.BlockSpec(shape, lambda i: (0,) * len(shape), **kw)

    return pl.pallas_call(
        _front_kernel,
        grid=(t // tm,),
        in_specs=[
            tok(d), const((1, d)),
            const(w_in_t.shape, pipeline_mode=pl.Buffered(1)),
            const(w_gate.shape, pipeline_mode=pl.Buffered(1)),
            const((1, n_gate)), tok(1), const((1, LANES)), const((1, LANES)), const((1, LANES)),
        ],
        out_specs=[
            tok(N_PROJ), tok(LANES), tok(n_gate),
            pl.BlockSpec((2, tm, HW), lambda i: (0, i, 0)),
            tok(HW),
        ],
        out_shape=[
            jax.ShapeDtypeStruct((t, N_PROJ), BF16),
            jax.ShapeDtypeStruct((t, LANES), F32),
            jax.ShapeDtypeStruct((t, n_gate), BF16),
            jax.ShapeDtypeStruct((2, t, HW), BF16),
            jax.ShapeDtypeStruct((t, HW), BF16),
        ],
        compiler_params=pltpu.CompilerParams(dimension_semantics=("parallel",),
                                             vmem_limit_bytes=FRONT_VMEM_LIMIT),
        name="front",
    )(x, g, w_in_t, w_gate, b_gate, pos, inv, gq, gk)


def _diff_attn_kernel(qz_ref, k_ref, v_ref, lam_ref, g_ref, o_ref, m_ref, acc_ref, v1_ref,
                      *, tq, tk, seg, sub, ahead, unroll, exp_dtype, lam_init):
    i = pl.program_id(2)
    n_sub = 2 * tq // sub

    @pl.when(i == 0)
    def _():
        v1_ref[:, 0:LANES] = v_ref[...]
        v1_ref[:, LANES:] = jnp.ones((v1_ref.shape[0], LANES), v1_ref.dtype)

    def scores(base, item):
        r, k0, nk = item[:3]
        mp, off = divmod(r * sub, tq)
        return _nt_dot(qz_ref[mp, off:off + sub, :], k_ref[pl.ds(base + k0, nk), :])

    def consume(base, item, s):
        r, k0, nk, masked, first = item
        off = (r * sub) % tq
        rs = slice(r * sub, (r + 1) * sub)
        if masked:
            row = lax.broadcasted_iota(jnp.int32, s.shape, 0) + off
            col = lax.broadcasted_iota(jnp.int32, s.shape, 1) + k0
            s = jnp.where((col // CHUNK) <= (row // CHUNK), s, -jnp.inf)
        m_cur = jnp.max(s, axis=-1, keepdims=True)
        m_new = jnp.broadcast_to(m_cur, (sub, LANES)) if first else jnp.maximum(m_ref[rs], m_cur)
        x = s - jnp.concatenate([m_new] * (nk // LANES), axis=1)
        p = jnp.exp2(x.astype(exp_dtype)).astype(BF16)
        pv = _dot(p, v1_ref[pl.ds(base + k0, nk), :])
        if first:
            acc_ref[rs] = pv
        else:
            alpha = jnp.exp2(m_ref[rs] - m_new)
            acc_ref[rs] = jnp.concatenate([alpha, alpha], axis=1) * acc_ref[rs] + pv
        m_ref[rs] = m_new

    def pipelined(base, items):
        pending = [scores(base, it) for it in items[:ahead]]
        for n, it in enumerate(items):
            s = pending.pop(0)
            if n + ahead < len(items):
                pending.append(scores(base, items[n + ahead]))
            consume(base, it, s)

    diag_items = []
    for kb in range(tq // seg):
        for r in range(n_sub):
            off = (r * sub) % tq
            k0, k1 = kb * seg, min((kb + 1) * seg, off + sub)
            if k1 > k0:
                diag_items.append((r, k0, k1 - k0, k1 > off, kb == 0))
    pipelined(pl.multiple_of(i * tq, tq), diag_items)

    def full_items(n_blocks):
        return [(r, kb * seg, seg, False, False)
                for kb in range(n_blocks * tk // seg) for r in range(n_sub)]

    n_full = i * (tq // tk)

    def body(j, carry):
        pipelined(pl.multiple_of(j * (unroll * tk), unroll * tk), full_items(unroll))
        return carry

    lax.fori_loop(0, n_full // unroll, body, 0)

    group = unroll // 2
    while group >= 1:
        done = (n_full // (2 * group)) * (2 * group)

        @pl.when(n_full % (2 * group) >= group)
        def _(group=group, done=done):
            pipelined(pl.multiple_of(done * tk, tk), full_items(group))

        group //= 2

    o = acc_ref[:, 0:LANES] / acc_ref[:, LANES:]
    lp = lam_ref[...]
    lam = (jnp.exp(jnp.sum(lp[0:1] * lp[1:2], axis=-1, keepdims=True))
           - jnp.exp(jnp.sum(lp[2:3] * lp[3:4], axis=-1, keepdims=True)) + lam_init)
    d = o[:tq] - lam * o[tq:]
    ms = jnp.mean(d * d, axis=-1, keepdims=True)
    o_ref[...] = (d * lax.rsqrt(ms + EPS) * g_ref[...] * (1.0 - lam_init)).astype(o_ref.dtype)


def _diff_attn(qz, kr, proj, lam_pack, g, *, batch, seq, tq, tk, seg, sub, ahead, unroll, exp_dtype,
               lam_init):
    t = batch * seq
    nq = seq // tq
    return pl.pallas_call(
        functools.partial(_diff_attn_kernel, tq=tq, tk=tk, seg=seg, sub=sub, ahead=ahead, unroll=unroll,
                          exp_dtype=exp_dtype, lam_init=lam_init),
        grid=(batch, DA_HEADS, nq),
        in_specs=[
            pl.BlockSpec((2, tq, LANES), lambda b, h, i: (0, b * nq + i, h)),
            pl.BlockSpec((seq, LANES), lambda b, h, i: (b, h)),
            pl.BlockSpec((seq, LANES), lambda b, h, i: (b, COL_DV // LANES + h)),
            pl.BlockSpec((8, LANES), lambda b, h, i: (0, 0)),
            pl.BlockSpec((1, LANES), lambda b, h, i: (0, 0)),
        ],
        out_specs=pl.BlockSpec((tq, LANES), lambda b, h, i: (b * nq + i, h)),
        out_shape=jax.ShapeDtypeStruct((t, HW), BF16),
        scratch_shapes=[
            pltpu.VMEM((2 * tq, LANES), F32),
            pltpu.VMEM((2 * tq, 2 * LANES), F32),
            pltpu.VMEM((seq, 2 * LANES), BF16),
        ],
        compiler_params=_cparams(("parallel", "parallel", "arbitrary")),
        name="diff_attn",
    )(qz, kr, proj, lam_pack, g)


def _gdn_prep_kernel(q_ref, k_ref, v_ref, qh_ref, kh_ref, vh_ref, ab_ref, wc_ref, alog_ref, dtb_ref,
                     pq_ref, c_ref, cd_ref, op_ref,
                     xp_ref, qn_ref, kn_ref, vn_ref, beta_ref, g_ref, grow_ref, *, ts, cpb):
    first = pl.program_id(1) == 0
    nc = ts // CHUNK

    for s, (cur_ref, halo_ref, dst_ref) in enumerate(
            ((q_ref, qh_ref, qn_ref), (k_ref, kh_ref, kn_ref), (v_ref, vh_ref, vn_ref))):
        halo = halo_ref[...].astype(F32)
        xp_ref[0:8, :] = jnp.where(first, 0.0, halo)
        xp_ref[8:, :] = cur_ref[...].astype(F32)
        acc = xp_ref[8:8 + ts, :] * wc_ref[CONV_K - 1:CONV_K, s * HW:(s + 1) * HW]
        for j in range(1, CONV_K):
            acc = acc + xp_ref[8 - j:8 - j + ts, :] * wc_ref[CONV_K - 1 - j:CONV_K - j, s * HW:(s + 1) * HW]
        half = 0.5 * acc
        act = half + half * jnp.tanh(half)
        if s < 2:
            for h in range(GDN_HEADS):
                sl = slice(h * LANES, (h + 1) * LANES)
                a_h = act[:, sl]
                dst_ref[:, sl] = a_h * lax.rsqrt(jnp.sum(a_h * a_h, axis=-1, keepdims=True) + EPS)
        else:
            dst_ref[...] = act

    ab = ab_ref[...]
    beta_ref[...] = jax.nn.sigmoid(ab)
    xa = ab + dtb_ref[...]
    softplus = jnp.maximum(xa, 0.0) + jnp.log1p(jnp.exp(-jnp.abs(xa)))
    la = -jnp.exp(alog_ref[...]) * softplus
    ti = lax.broadcasted_iota(jnp.int32, (ts, ts), 0)
    tj = lax.broadcasted_iota(jnp.int32, (ts, ts), 1)
    ltri = jnp.where((tj <= ti) & ((ti // CHUNK) == (tj // CHUNK)), 1.0, 0.0).astype(BF16)
    g_all = sum(_dot(ltri, part) for part in _split_bf16(la))
    g_ref[...] = g_all
    sel_r = lax.broadcasted_iota(jnp.int32, (8, LANES), 0)
    sel_c = lax.broadcasted_iota(jnp.int32, (8, LANES), 1)
    sel = jnp.where(sel_c == sel_r + AB_A0, 1.0, 0.0).astype(BF16)
    grow_all = sum(_nt_dot(sel, part) for part in _split_bf16(g_all))
    for c in range(nc):
        grow_ref[c] = grow_all[:, c * CHUNK:(c + 1) * CHUNK]

    ri = lax.broadcasted_iota(jnp.int32, (CHUNK, CHUNK), 0)
    ci = lax.broadcasted_iota(jnp.int32, (CHUNK, CHUNK), 1)
    incl = ci <= ri
    strict = ci < ri
    scale = GDN_K_DIM ** -0.5

    eye = jnp.where(ri == ci, 1.0, 0.0)
    same_blk = (ri // 16) == (ci // 16)

    def bdot(a, b):
        return _dot(a.astype(BF16), b.astype(BF16))

    def group_body(cg, carry):
        us = []
        for cc in range(cpb):
            c = cg * cpb + cc
            r0 = pl.multiple_of(c * CHUNK, CHUNK)
            rows = pl.ds(r0, CHUNK)
            for h in range(GDN_HEADS):
                sl = slice(h * LANES, (h + 1) * LANES)
                gcol = g_ref[rows, AB_A0 + h:AB_A0 + h + 1]
                glast = g_ref[pl.ds(r0 + CHUNK - 1, 1), AB_A0 + h:AB_A0 + h + 1]
                grow = grow_ref[c, h:h + 1, :]
                us.append(dict(
                    c=c, h=h, rows=rows, sl=sl, gcol=gcol, glast=glast,
                    q=qn_ref[rows, sl] * scale, k=kn_ref[rows, sl], v=vn_ref[rows, sl],
                    b=beta_ref[rows, AB_B0 + h:AB_B0 + h + 1], eg=jnp.exp(gcol),
                    decay=jnp.where(incl, jnp.exp(jnp.where(incl, gcol - grow, 0.0)), 0.0)))

        for u in us:
            kb = u["k"].astype(BF16)
            u["qk_kk"] = _nt_dot(jnp.concatenate([u["q"].astype(BF16), kb], axis=0), kb)
        for u in us:
            a = jnp.where(strict, u["b"] * u["qk_kk"][CHUNK:] * u["decay"], 0.0)
            ad = jnp.where(same_blk, a, 0.0)
            u["ao"] = a - ad
            u["x"] = eye - ad
            u["ad"] = ad
        for u in us:
            u["p"] = bdot(u["ad"], u["ad"])
        for _ in range(2):
            for u in us:
                xp = bdot(u["x"], u["p"])
                u["p"] = bdot(u["p"], u["p"])
                u["x"] = u["x"] + xp
        for u in us:
            u["x"] = u["x"] + bdot(u["x"], u["p"])
        for u in us:
            rhs = jnp.concatenate([u["k"] * (u["b"] * u["eg"]), u["v"] * u["b"]], axis=1)
            u["bm"] = bdot(u["x"], u["ao"])
            u["z"] = bdot(u["x"], rhs)
        for u in us:
            u["b2"] = bdot(u["bm"], u["bm"])
        for u in us:
            u["z"] = u["z"] + bdot(u["b2"], u["z"])
        for u in us:
            u["wu"] = (u["z"] - bdot(u["bm"], u["z"])).astype(BF16)
        for u in us:
            qkm = (u["qk_kk"][:CHUNK] * u["decay"]).astype(BF16)
            kd = (u["k"] * jnp.exp(u["glast"] - u["gcol"])).astype(BF16)
            u["r2"] = _dot(qkm, u["wu"])
            u["r3"] = _tn_dot(kd, u["wu"])
        for u in us:
            c, h = u["c"], u["h"]
            pq_ref[c, h, 0:LANES, :] = u["r3"][:, :LANES].astype(pq_ref.dtype)
            pq_ref[c, h, LANES:, :] = (u["q"] * u["eg"] - u["r2"][:, :LANES]).astype(pq_ref.dtype)
            c_ref[c, h] = u["r3"][:, LANES:].astype(c_ref.dtype)
            cd_ref[c, h] = jnp.broadcast_to(jnp.exp(u["glast"]), (8, LANES))
            op_ref[u["rows"], u["sl"]] = u["r2"][:, LANES:].astype(op_ref.dtype)
        return carry

    lax.fori_loop(0, nc // cpb, group_body, 0)


def _gdn_prep(proj, ab, wconv, alog_row, dtb_row, *, batch, seq, ts, cpb):
    t = batch * seq
    nt = seq // ts
    nc = ts // CHUNK
    n_chunks = t // CHUNK
    hb = ts // 8

    def cur(col):
        return pl.BlockSpec((ts, HW), lambda b, i: (b * nt + i, col // HW))

    def halo(col):
        return pl.BlockSpec((8, HW), lambda b, i: (jnp.maximum((b * nt + i) * hb - 1, 0), col // HW))

    return pl.pallas_call(
        functools.partial(_gdn_prep_kernel, ts=ts, cpb=cpb),
        grid=(batch, nt),
        in_specs=[
            cur(COL_GQ), cur(COL_GK), cur(COL_GV), halo(COL_GQ), halo(COL_GK), halo(COL_GV),
            pl.BlockSpec((ts, LANES), lambda b, i: (b * nt + i, 0)),
            pl.BlockSpec((CONV_K, 3 * HW), lambda b, i: (0, 0)),
            pl.BlockSpec((1, LANES), lambda b, i: (0, 0)),
            pl.BlockSpec((1, LANES), lambda b, i: (0, 0)),
        ],
        out_specs=[
            pl.BlockSpec((nc, GDN_HEADS, GDN_K_DIM + CHUNK, LANES), lambda b, i: (b * nt + i, 0, 0, 0)),
            pl.BlockSpec((nc, GDN_HEADS, GDN_K_DIM, LANES), lambda b, i: (b * nt + i, 0, 0, 0)),
            pl.BlockSpec((nc, GDN_HEADS, 8, LANES), lambda b, i: (b * nt + i, 0, 0, 0)),
            pl.BlockSpec((ts, HW), lambda b, i: (b * nt + i, 0)),
        ],
        out_shape=[
            jax.ShapeDtypeStruct((n_chunks, GDN_HEADS, GDN_K_DIM + CHUNK, LANES), BF16),
            jax.ShapeDtypeStruct((n_chunks, GDN_HEADS, GDN_K_DIM, LANES), BF16),
            jax.ShapeDtypeStruct((n_chunks, GDN_HEADS, 8, LANES), F32),
            jax.ShapeDtypeStruct((t, HW), BF16),
        ],
        scratch_shapes=[
            pltpu.VMEM((ts + 8, HW), F32),
            pltpu.VMEM((ts, HW), F32),
            pltpu.VMEM((ts, HW), F32),
            pltpu.VMEM((ts, HW), F32),
            pltpu.VMEM((ts, LANES), F32),
            pltpu.VMEM((ts, LANES), F32),
            pltpu.VMEM((nc, 8, CHUNK), F32),
        ],
        compiler_params=_cparams(("parallel", "parallel")),
        name="gdn_prep",
    )(proj, proj, proj, proj, proj, proj, ab, wconv, alog_row, dtb_row)


def _gdn_scan_kernel(pq_ref, c_ref, cd_ref, op_ref, o_ref, s_ref, *, ts):
    @pl.when(pl.program_id(0) == 0)
    def _():
        s_ref[...] = jnp.zeros(s_ref.shape, F32)

    nc = ts // CHUNK
    batch = s_ref.shape[0]

    def chunk_body(c, carry):
        r0 = pl.multiple_of(c * CHUNK, CHUNK)
        rows = pl.ds(r0, CHUNK)
        for b in range(batch):
            for h in range(GDN_HEADS):
                sl = slice(h * LANES, (h + 1) * LANES)
                s = s_ref[b, h]
                r = _dot(pq_ref[b, c, h], s.astype(BF16))
                o = r[GDN_K_DIM:] + op_ref[b, rows, sl].astype(F32)
                s_ref[b, h] = cd_ref[b, c, h, 0:1, :] * s - r[:GDN_K_DIM] + c_ref[b, c, h].astype(F32)
                o_ref[b, rows, sl] = o.astype(o_ref.dtype)
        return carry

    lax.fori_loop(0, nc, chunk_body, 0)


def _gdn_scan(pq, cmat, cd, op, *, batch, seq, ts):
    nt = seq // ts
    nc = ts // CHUNK
    n = seq // CHUNK

    def per_chunk(a):
        return a.reshape((batch, n) + a.shape[1:])

    def chunk_blk(rows):
        return pl.BlockSpec((batch, nc, GDN_HEADS, rows, LANES), lambda i: (0, i, 0, 0, 0))

    def tok_blk(col):
        return pl.BlockSpec((batch, ts, HW), lambda i: (0, i, col))

    out = pl.pallas_call(
        functools.partial(_gdn_scan_kernel, ts=ts),
        grid=(nt,),
        in_specs=[
            chunk_blk(GDN_K_DIM + CHUNK), chunk_blk(GDN_K_DIM), chunk_blk(8),
            tok_blk(0),
        ],
        out_specs=tok_blk(0),
        out_shape=jax.ShapeDtypeStruct((batch, seq, HW), BF16),
        scratch_shapes=[pltpu.VMEM((batch, GDN_HEADS, GDN_K_DIM, GDN_V_DIM), F32)],
        compiler_params=_cparams(("arbitrary",)),
        name="gdn_scan",
    )(per_chunk(pq), per_chunk(cmat), per_chunk(cd), op.reshape(batch, seq, HW))
    return out.reshape(batch * seq, HW)


def _mem_kv_kernel(m_ref, g_ref, w_ref, gk_ref, k_ref, v_ref):
    m = m_ref[...]
    mn = m * lax.rsqrt(jnp.mean(m * m, axis=-1, keepdims=True) + EPS) * g_ref[...]
    kv = _dot(mn.astype(BF16), w_ref[...])
    for h in range(CA_HEADS):
        sl = slice(h * LANES, (h + 1) * LANES)
        kh = kv[:, sl]
        k_ref[:, sl] = (kh * lax.rsqrt(jnp.mean(kh * kh, axis=-1, keepdims=True) + EPS)
                        * gk_ref[...]).astype(k_ref.dtype)
    v_ref[...] = kv[:, CA_Q:].astype(v_ref.dtype)


def _mem_kv(mem2, g, w, gk, *, batch, n_mem):
    d = mem2.shape[1]
    return pl.pallas_call(
        _mem_kv_kernel,
        grid=(batch,),
        in_specs=[
            pl.BlockSpec((n_mem, d), lambda b: (b, 0)),
            pl.BlockSpec((1, d), lambda b: (0, 0)),
            pl.BlockSpec((d, 2 * CA_Q), lambda b: (0, 0)),
            pl.BlockSpec((1, LANES), lambda b: (0, 0)),
        ],
        out_specs=[
            pl.BlockSpec((n_mem, CA_Q), lambda b: (b, 0)),
            pl.BlockSpec((n_mem, CA_Q), lambda b: (b, 0)),
        ],
        out_shape=[
            jax.ShapeDtypeStruct((batch * n_mem, CA_Q), BF16),
            jax.ShapeDtypeStruct((batch * n_mem, CA_Q), BF16),
        ],
        compiler_params=_cparams(("parallel",)),
        name="mem_kv",
    )(mem2, g, w, gk)


def _back_kernel(x_ref, oa_ref, ob_ref, z_ref, gn_ref, cq_ref, gt_ref, kc_ref, vc_ref, gcq_ref,
                 wa_ref, wb_ref, wc_ref, wo_ref, g2_ref, w1_ref, w2_ref, o_ref, *, tf):
    d = x_ref.shape[1]
    gdn = []
    for h in range(GDN_HEADS):
        sl = slice(h * LANES, (h + 1) * LANES)
        o = ob_ref[:, sl].astype(F32)
        on = o * lax.rsqrt(jnp.mean(o * o, axis=-1, keepdims=True) + EPS) * gn_ref[...]
        z = z_ref[:, sl].astype(F32)
        gdn.append((on * (z * jax.nn.sigmoid(z))).astype(BF16))
    ob = jnp.concatenate(gdn, axis=1)
    heads = []
    for h in range(CA_HEADS):
        sl = slice(h * LANES, (h + 1) * LANES)
        q = cq_ref[:, sl].astype(F32)
        qn = (q * lax.rsqrt(jnp.mean(q * q, axis=-1, keepdims=True) + EPS) * gcq_ref[...]
              * (CA_DIM ** -0.5))
        s = _nt_dot(qn.astype(BF16), kc_ref[:, sl])
        p = jnp.exp(s - jnp.max(s, axis=-1, keepdims=True))
        l = jnp.sum(p, axis=-1, keepdims=True)
        heads.append((_dot(p.astype(BF16), vc_ref[:, sl]) / l).astype(BF16))
    oc = jnp.concatenate(heads, axis=1)
    ya = _dot(oa_ref[...], wa_ref[...])
    yb = _dot(ob, wb_ref[...])
    yc = _dot(oc, wc_ref[...])
    mixed = (gt_ref[:, 0:d].astype(F32) * ya + gt_ref[:, d:2 * d].astype(F32) * yb
             + gt_ref[:, 2 * d:3 * d].astype(F32) * yc)
    x1 = x_ref[...] + _dot(mixed.astype(BF16), wo_ref[...])
    ms = jnp.mean(x1 * x1, axis=-1, keepdims=True)
    h2 = (x1 * lax.rsqrt(ms + EPS) * g2_ref[...]).astype(BF16)
    o_ref[...] = x1
    for c0 in range(0, w1_ref.shape[1], tf):
        a = jnp.maximum(_dot(h2, w1_ref[:, c0:c0 + tf]), 0.0)
        o_ref[...] += _dot((a * a).astype(BF16), w2_ref[c0:c0 + tf, :])


def _back(x, oa, ob, gn, proj, gates, kc, vc, gcq, wa, wb, wc, wo, g2, w1, w2, *, seq, n_mem, tm, tf):
    t, d = x.shape
    tiles_per_seq = seq // tm

    def tok(n, col=0):
        return pl.BlockSpec((tm, n), lambda i: (i, col))

    def const(a):
        return pl.BlockSpec(a.shape, lambda i: (0, 0), pipeline_mode=pl.Buffered(1))

    def mem_blk():
        return pl.BlockSpec((n_mem, CA_Q), lambda i: (i // tiles_per_seq, 0))

    return pl.pallas_call(
        functools.partial(_back_kernel, tf=tf),
        grid=(t // tm,),
        in_specs=[
            tok(d), tok(HW), tok(HW), tok(HW, COL_GZ // HW), const(gn), tok(CA_Q, COL_CQ // CA_Q),
            tok(N_BRANCH * d),
            mem_blk(), mem_blk(), const(gcq), const(wa), const(wb), const(wc), const(wo), const(g2),
            const(w1), const(w2),
        ],
        out_specs=tok(d),
        out_shape=jax.ShapeDtypeStruct((t, d), F32),
        compiler_params=pltpu.CompilerParams(dimension_semantics=("parallel",),
                                             vmem_limit_bytes=BACK_VMEM_LIMIT),
        name="back",
    )(x, oa, ob, proj, gn, proj, gates, kc, vc, gcq, wa, wb, wc, wo, g2, w1, w2)


def _pick(n, pref):
    t = min(pref, n)
    while n % t:
        t //= 2
    return t


def _row(v, width=None):
    v = v.astype(F32).reshape(1, -1)
    if width is not None and v.shape[1] < width:
        v = jnp.pad(v, ((0, 0), (0, width - v.shape[1])))
    return v


def _layer(x2, mem2, pos2, inv_row, batch, seq, n_mem, l, p):
    t, d = x2.shape
    (ln1_g, w_in, w_gate, b_gate, da_qnorm_g, da_knorm_g, lq1, lk1, lq2, lk2, da_subln_g, w_o_diff,
     w_conv, gdn_a_log, gdn_dt_bias, gdn_norm_g, w_o_delta, mem_norm_g, w_mem_kv, ca_qnorm_g,
     ca_knorm_g, w_o_cross, w_out, ln2_g, w_mlp1, w_mlp2) = p

    o = [0]
    for n in (HW, HW, HW, HW, HW, HW, HW, GDN_HEADS, GDN_HEADS, CA_Q):
        o.append(o[-1] + n)
    w_t = w_in.T
    sec = [w_t[o[i]:o[i + 1]] for i in range(10)]
    w_ab = jnp.zeros((MXU_N, d), F32)
    w_ab = w_ab.at[AB_B0:AB_B0 + GDN_HEADS].set(sec[7]).at[AB_A0:AB_A0 + GDN_HEADS].set(sec[8])
    w_in_t = jnp.concatenate([sec[3], sec[4], sec[5], sec[6], sec[9], sec[2], sec[0], sec[1], w_ab],
                             axis=0).astype(BF16)
    gq = _row(jnp.tile(da_qnorm_g, 2))
    gk = _row(jnp.tile(da_knorm_g, 2))
    proj, ab, gates, qz, kr = _front(x2, _row(ln1_g), w_in_t, w_gate.astype(BF16), _row(b_gate), pos2,
                                     inv_row, gq, gk, tm=_pick(t, 512))

    lam_pack = jnp.zeros((8, LANES), F32)
    lam_pack = lam_pack.at[0:4, 0:DA_QK_DIM].set(jnp.stack([lq1, lk1, lq2, lk2]).astype(F32))
    lam_init = 0.8 - 0.6 * math.exp(-0.3 * l)
    oa = _diff_attn(qz, kr, proj, lam_pack, _row(da_subln_g), batch=batch, seq=seq,
                    tq=_pick(seq, 1024), tk=_pick(seq, 1024), seg=_pick(seq, 512), sub=256, ahead=3, unroll=4,
                    exp_dtype=BF16, lam_init=lam_init)

    alog_row = jnp.zeros((1, LANES), F32).at[0, AB_A0:AB_A0 + GDN_HEADS].set(gdn_a_log.astype(F32))
    dtb_row = jnp.zeros((1, LANES), F32).at[0, AB_A0:AB_A0 + GDN_HEADS].set(gdn_dt_bias.astype(F32))
    pq, cmat, cd, op = _gdn_prep(proj, ab, w_conv.astype(F32), alog_row, dtb_row,
                                 batch=batch, seq=seq, ts=_pick(seq, 256), cpb=4)
    ob = _gdn_scan(pq, cmat, cd, op, batch=batch, seq=seq, ts=_pick(seq, 256))

    kc, vc = _mem_kv(mem2, _row(mem_norm_g), w_mem_kv.astype(BF16), _row(ca_knorm_g),
                     batch=batch, n_mem=n_mem)

    return _back(x2, oa, ob, _row(gdn_norm_g), proj, gates, kc, vc, _row(ca_qnorm_g), w_o_diff.astype(BF16),
                 w_o_delta.astype(BF16), w_o_cross.astype(BF16), w_out.astype(BF16), _row(ln2_g),
                 w_mlp1.astype(BF16), w_mlp2.astype(BF16), seq=seq, n_mem=n_mem,
                 tm=_pick(seq, 512), tf=1024)


def kernel(x, mem, positions, ln1_g, w_in, w_gate, b_gate, da_qnorm_g, da_knorm_g, da_lambda_q1, da_lambda_k1, da_lambda_q2, da_lambda_k2, da_subln_g, w_o_diff, w_conv, gdn_a_log, gdn_dt_bias, gdn_norm_g, w_o_delta, mem_norm_g, w_mem_kv, ca_qnorm_g, ca_knorm_g, w_o_cross, w_out, ln2_g, w_mlp1, w_mlp2):
    batch, seq, d = x.shape
    n_mem = mem.shape[1]
    x2 = x.reshape(batch * seq, d)
    mem2 = mem.reshape(batch * n_mem, d)
    pos2 = positions.reshape(batch * seq, 1)
    half = DA_QK_DIM // 2
    inv = jnp.exp(-math.log(ROPE_THETA) * jnp.arange(half, dtype=F32) / half)
    inv_row = jnp.tile(inv, LANES // half).reshape(1, LANES)
    params = (ln1_g, w_in, w_gate, b_gate, da_qnorm_g, da_knorm_g, da_lambda_q1, da_lambda_k1,
              da_lambda_q2, da_lambda_k2, da_subln_g, w_o_diff, w_conv, gdn_a_log, gdn_dt_bias,
              gdn_norm_g, w_o_delta, mem_norm_g, w_mem_kv, ca_qnorm_g, ca_knorm_g, w_o_cross, w_out,
              ln2_g, w_mlp1, w_mlp2)
    for l in range(ln1_g.shape[0]):
        x2 = _layer(x2, mem2, pos2, inv_row, batch, seq, n_mem, l, tuple(a[l] for a in params))
    return x2.reshape(batch, seq, d)
```
